```python
import jax, jax.numpy as jnp
from jax import lax
import numpy as np

D_MODEL = 1024
BATCH = 8
SEQ = 4096
DEPTH = 1

CHUNK = 64
EPS = 1e-6
SSD_EXPAND = 2
SSD_INNER = SSD_EXPAND * D_MODEL
SSD_HEAD_DIM = 64
SSD_HEADS = SSD_INNER // SSD_HEAD_DIM
SSD_GROUPS = 4
SSD_HPG = SSD_HEADS // SSD_GROUPS
SSD_STATE = 128
SSD_CONV = 4
SSD_XBC = SSD_INNER + 2 * SSD_GROUPS * SSD_STATE
SCONV_DIM = D_MODEL
SCONV_K = 3
FFN_HIDDEN = ((8 * D_MODEL + 767) // 768) * 256
IN_SIZES = (SSD_INNER, SSD_XBC, SSD_HEADS, SCONV_DIM, SCONV_DIM, SCONV_DIM, D_MODEL, D_MODEL)
IN_DIM = sum(IN_SIZES)
IN_SPLITS = tuple(int(v) for v in np.cumsum(IN_SIZES)[:-1])

kernel_name = 'hybrid_ssd_shortconv_gated_block'


def rmsnorm(x, w):
    x32 = x.astype(jnp.float32)
    y = x32 * lax.rsqrt(jnp.mean(x32 * x32, axis=-1, keepdims=True) + EPS)
    return (y * w.astype(jnp.float32)).astype(x.dtype)


def causal_depthwise_conv(x, w):
    k = w.shape[0]
    return lax.conv_general_dilated(
        x, w[:, None, :].astype(x.dtype), window_strides=(1,), padding=[(k - 1, 0)],
        dimension_numbers=('NWC', 'WIO', 'NWC'), feature_group_count=x.shape[-1])


def ssd_chunked_scan(x, dt, a, bm, cm):
    bsz, seqlen = x.shape[0], x.shape[1]
    nc = seqlen // CHUNK

    def to_chunks(t):
        return jnp.moveaxis(t.reshape((bsz, nc, CHUNK) + t.shape[2:]), 1, 0)

    causal = jnp.tril(jnp.ones((CHUNK, CHUNK), dtype=bool))[None, :, :, None, None]

    def step(state, inp):
        xc, dtc, bc, cc = inp
        acum = jnp.cumsum(dtc * a, axis=1)
        seg = acum[:, :, None] - acum[:, None, :]
        decay = jnp.exp(jnp.where(causal, seg, -jnp.inf))
        cb = jnp.einsum('blgn,bsgn->blsg', cc, bc)
        xdt = xc * dtc[..., None]
        scores = cb[..., None] * decay
        y_diag = jnp.einsum('blsgr,bsgrp->blgrp', scores, xdt)
        y_off = jnp.einsum('blgn,bgrpn->blgrp', cc, state) * jnp.exp(acum)[..., None]
        to_end = jnp.exp(acum[:, -1:] - acum)
        new_state = (state * jnp.exp(acum[:, -1])[..., None, None]
                     + jnp.einsum('bsgn,bsgr,bsgrp->bgrpn', bc, to_end, xdt))
        return new_state, y_diag + y_off

    state0 = jnp.zeros((bsz, SSD_GROUPS, SSD_HPG, SSD_HEAD_DIM, SSD_STATE), jnp.float32)
    _, ys = lax.scan(step, state0, (to_chunks(x), to_chunks(dt), to_chunks(bm), to_chunks(cm)))
    return jnp.moveaxis(ys, 0, 1).reshape(x.shape)


def hybrid_mixer(h, w_in, ssd_conv_w, ssd_conv_b, dt_bias, a_log, d_skip, ssd_norm_w,
                 w_ssd_proj, sconv_w, w_sconv_proj, w_o):
    bsz, seqlen, _ = h.shape
    u = h @ w_in
    z, xbc, dt_raw, s_b, s_c, s_x, g_a, g_b = jnp.split(u, IN_SPLITS, axis=-1)

    xbc = jax.nn.silu(causal_depthwise_conv(xbc, ssd_conv_w) + ssd_conv_b)
    xs, bm, cm = jnp.split(xbc, (SSD_INNER, SSD_INNER + SSD_GROUPS * SSD_STATE), axis=-1)
    xs32 = xs.astype(jnp.float32).reshape(bsz, seqlen, SSD_GROUPS, SSD_HPG, SSD_HEAD_DIM)
    dt = jax.nn.softplus(dt_raw.astype(jnp.float32) + dt_bias.astype(jnp.float32))
    dt = dt.reshape(bsz, seqlen, SSD_GROUPS, SSD_HPG)
    a = -jnp.exp(a_log.astype(jnp.float32)).reshape(SSD_GROUPS, SSD_HPG)
    bm32 = bm.astype(jnp.float32).reshape(bsz, seqlen, SSD_GROUPS, SSD_STATE)
    cm32 = cm.astype(jnp.float32).reshape(bsz, seqlen, SSD_GROUPS, SSD_STATE)
    y = ssd_chunked_scan(xs32, dt, a, bm32, cm32)
    y = y + d_skip.astype(jnp.float32).reshape(SSD_GROUPS, SSD_HPG)[..., None] * xs32
    y = y.reshape(bsz, seqlen, SSD_INNER) * jax.nn.silu(z.astype(jnp.float32))
    yg = y.reshape(bsz, seqlen, SSD_GROUPS, SSD_INNER // SSD_GROUPS)
    yg = yg * lax.rsqrt(jnp.mean(yg * yg, axis=-1, keepdims=True) + EPS)
    y = (yg.reshape(bsz, seqlen, SSD_INNER) * ssd_norm_w.astype(jnp.float32)).astype(h.dtype)
    branch_a = y @ w_ssd_proj

    v = causal_depthwise_conv(s_c * s_x, sconv_w)
    branch_b = (s_b * v) @ w_sconv_proj

    merged = jax.nn.sigmoid(g_a) * branch_a + jax.nn.sigmoid(g_b) * branch_b
    return merged @ w_o


def swiglu(h, w_gate, w_up, w_down):
    return (jax.nn.silu(h @ w_gate) * (h @ w_up)) @ w_down


def setup_inputs(seed: int = 0) -> dict:
    key = jax.random.key(seed)
    ks = jax.random.split(key, 20)
    f32 = jnp.float32

    def nrm(k, shape, scale):
        return jax.random.normal(k, shape, f32) * scale

    x = jax.random.normal(ks[0], (BATCH, SEQ, D_MODEL), f32)
    dt_init = jnp.exp(jax.random.uniform(ks[5], (DEPTH, SSD_HEADS), f32, np.log(1e-3), np.log(1e-1)))
    dt_bias = dt_init + jnp.log(-jnp.expm1(-dt_init))
    return {
        'x': x,
        'norm_mix_w': 1.0 + nrm(ks[1], (DEPTH, D_MODEL), 0.05),
        'w_in': nrm(ks[2], (DEPTH, D_MODEL, IN_DIM), D_MODEL ** -0.5),
        'ssd_conv_w': nrm(ks[3], (DEPTH, SSD_CONV, SSD_XBC), SSD_CONV ** -0.5),
        'ssd_conv_b': nrm(ks[4], (DEPTH, SSD_XBC), 0.01),
        'dt_bias': dt_bias,
        'a_log': jnp.log(jax.random.uniform(ks[6], (DEPTH, SSD_HEADS), f32, 1.0, 16.0)),
        'd_skip': 1.0 + nrm(ks[7], (DEPTH, SSD_HEADS), 0.1),
        'ssd_norm_w': 1.0 + nrm(ks[8], (DEPTH, SSD_INNER), 0.05),
        'w_ssd_proj': nrm(ks[9], (DEPTH, SSD_INNER, D_MODEL), SSD_INNER ** -0.5),
        'sconv_w': nrm(ks[10], (DEPTH, SCONV_K, SCONV_DIM), SCONV_K ** -0.5),
        'w_sconv_proj': nrm(ks[11], (DEPTH, SCONV_DIM, D_MODEL), SCONV_DIM ** -0.5),
        'w_o': nrm(ks[12], (DEPTH, D_MODEL, D_MODEL), D_MODEL ** -0.5),
        'norm_ffn_w': 1.0 + nrm(ks[13], (DEPTH, D_MODEL), 0.05),
        'w_gate': nrm(ks[14], (DEPTH, D_MODEL, FFN_HIDDEN), D_MODEL ** -0.5),
        'w_up': nrm(ks[15], (DEPTH, D_MODEL, FFN_HIDDEN), D_MODEL ** -0.5),
        'w_down': nrm(ks[16], (DEPTH, FFN_HIDDEN, D_MODEL), FFN_HIDDEN ** -0.5),
        'final_norm_w': 1.0 + nrm(ks[17], (D_MODEL,), 0.05),
    }


def reference(x, norm_mix_w, w_in, ssd_conv_w, ssd_conv_b, dt_bias, a_log, d_skip, ssd_norm_w,
              w_ssd_proj, sconv_w, w_sconv_proj, w_o, norm_ffn_w, w_gate, w_up, w_down,
              final_norm_w):
    h = x
    for i in range(DEPTH):
        h = h + hybrid_mixer(rmsnorm(h, norm_mix_w[i]), w_in[i], ssd_conv_w[i], ssd_conv_b[i],
                             dt_bias[i], a_log[i], d_skip[i], ssd_norm_w[i], w_ssd_proj[i],
                             sconv_w[i], w_sconv_proj[i], w_o[i])
        h = h + swiglu(rmsnorm(h, norm_ffn_w[i]), w_gate[i], w_up[i], w_down[i])
    return rmsnorm(h, final_norm_w)
```

```python
import functools

import jax
import jax.numpy as jnp
from jax import lax
from jax.experimental import pallas as pl
from jax.experimental.pallas import tpu as pltpu

F32 = jnp.float32
BF16 = jnp.bfloat16

EPS = 1e-6
D_MODEL = 1024
SSD_INNER = 2048
SSD_HEAD_DIM = 64
SSD_HEADS = 32
SSD_GROUPS = 4
SSD_STATE = 128
SSD_CONV = 4
SSD_XBC = SSD_INNER + 2 * SSD_GROUPS * SSD_STATE
SCONV_K = 3
CHUNK = 64
LANES = 128
HEADS_PER_BLOCK = 4
BLOCK = HEADS_PER_BLOCK * SSD_HEAD_DIM
N_BLOCKS = SSD_INNER // BLOCK
CARRY = 8

O_Z = 0
O_XBC = O_Z + SSD_INNER
O_SB = O_XBC + SSD_XBC
O_SC = O_SB + D_MODEL
O_SX = O_SC + D_MODEL
O_GA = O_SX + D_MODEL
O_GB = O_GA + D_MODEL
O_DT = O_GB + D_MODEL
IN_DIM_PAD = O_DT + LANES

MIXER_TS = 256
FFN_TM = 512
VMEM_LIMIT = 58 * 1024 * 1024


def _softplus(v):
    return jnp.maximum(v, 0.0) + jnp.log(1.0 + jnp.exp(-jnp.abs(v)))


def _split2(v):
    hi = v.astype(BF16)
    lo = (v - hi.astype(F32)).astype(BF16)
    return hi, lo


def _mixer_kernel(x_ref, nw_ref, win_ref, cw_ref, cb_ref, dtb_ref, alog_ref, dskip_ref, snw_ref,
                  wssd_ref, scw_ref, wsc_ref, wo_ref, e2_ref, out_ref,
                  xbc_buf, p_buf, zs_ref, xs_ref, b_ref, c_ref, dt_ref, y_ref, yn_ref, sbv_ref,
                  ga_ref, gb_ref, state_ref, *, ts):
    j = pl.program_id(1)

    @pl.when(j == 0)
    def _():
        xbc_buf[0:CARRY, :] = jnp.zeros((CARRY, SSD_XBC), F32)
        p_buf[0:CARRY, :] = jnp.zeros((CARRY, D_MODEL), F32)
        state_ref[...] = jnp.zeros_like(state_ref)

    x = x_ref[...]
    xn = (x * lax.rsqrt(jnp.mean(x * x, axis=-1, keepdims=True) + EPS) * nw_ref[...]).astype(BF16)

    def proj(lo, hi):
        return jnp.dot(xn, win_ref[:, lo:hi], preferred_element_type=F32)

    for k in range(2):
        zs_ref[:, k * 1024:(k + 1) * 1024] = jax.nn.silu(proj(O_Z + k * 1024, O_Z + (k + 1) * 1024))
    for k in range(3):
        xbc_buf[CARRY:CARRY + ts, k * 1024:(k + 1) * 1024] = proj(O_XBC + k * 1024, O_XBC + (k + 1) * 1024)

    for k in range(6):
        cols = slice(k * 512, (k + 1) * 512)
        acc = cb_ref[:, cols]
        for i in range(SSD_CONV):
            r0 = CARRY - (SSD_CONV - 1) + i
            acc = acc + cw_ref[i:i + 1, cols] * xbc_buf[r0:r0 + ts, cols]
        act = jax.nn.silu(acc)
        if k < 4:
            xs_ref[:, cols] = act
        elif k == 4:
            b_ref[...] = act.astype(BF16)
        else:
            c_ref[...] = act.astype(BF16)
    xbc_buf[0:CARRY, :] = xbc_buf[ts:ts + CARRY, :]

    p_buf[CARRY:CARRY + ts, :] = proj(O_SC, O_SC + D_MODEL) * proj(O_SX, O_SX + D_MODEL)
    v = None
    for i in range(SCONV_K):
        r0 = CARRY - (SCONV_K - 1) + i
        term = scw_ref[i:i + 1, :] * p_buf[r0:r0 + ts, :]
        v = term if v is None else v + term
    p_buf[0:CARRY, :] = p_buf[ts:ts + CARRY, :]
    sbv_ref[...] = (proj(O_SB, O_SB + D_MODEL) * v).astype(BF16)

    ga_ref[...] = jax.nn.sigmoid(proj(O_GA, O_GA + D_MODEL))
    gb_ref[...] = jax.nn.sigmoid(proj(O_GB, O_GB + D_MODEL))
    dt_ref[...] = _softplus(proj(O_DT, O_DT + LANES) + dtb_ref[...])

    a_row = -jnp.exp(alog_ref[...])
    row64 = lax.broadcasted_iota(jnp.int32, (CHUNK, CHUNK), 0)
    col64 = lax.broadcasted_iota(jnp.int32, (CHUNK, CHUNK), 1)
    tril = (row64 >= col64).astype(BF16)
    rowb = lax.broadcasted_iota(jnp.int32, (CHUNK, BLOCK), 0)
    laneb = lax.broadcasted_iota(jnp.int32, (CHUNK, BLOCK), 1)
    pos = laneb % SSD_HEAD_DIM
    causal = rowb >= pos
    ident = rowb == pos
    lane_head = laneb // SSD_HEAD_DIM

    def chunk(c, carry):
        r0 = pl.multiple_of(c * CHUNK, CHUNK)
        rows = pl.ds(r0, CHUNK)
        dt_c = dt_ref[rows, :]
        da = dt_c * a_row
        hi = da.astype(BF16)
        r1 = da - hi.astype(F32)
        mid = r1.astype(BF16)
        lo = (r1 - mid.astype(F32)).astype(BF16)
        cs = jnp.dot(tril, jnp.concatenate([hi, mid, lo], axis=1), preferred_element_type=F32)
        acum = cs[:, 0:LANES] + cs[:, LANES:2 * LANES] + cs[:, 2 * LANES:3 * LANES]
        ah, al = _split2(acum)
        dh, dl = _split2(dt_c)
        lhs = jnp.concatenate([jnp.concatenate([ah, al], axis=1),
                               jnp.concatenate([dh, dl], axis=1)], axis=0)

        for g in range(SSD_GROUPS):
            gcols = slice(g * SSD_STATE, (g + 1) * SSD_STATE)
            c_g = c_ref[rows, gcols]
            b_g = b_ref[rows, gcols]
            cb2 = lax.dot_general(c_g, jnp.concatenate([b_g, b_g], axis=0),
                                  (((1,), (1,)), ((), ())), preferred_element_type=F32)
            cb4 = jnp.concatenate([cb2, cb2], axis=1)
            for half in range(2):
                blk = g * 2 + half
                cols = slice(blk * BLOCK, (blk + 1) * BLOCK)
                full = jnp.dot(lhs, e2_ref[:, cols], preferred_element_type=F32)
                a_f = full[0:CHUNK]
                dt_f = full[CHUNK:2 * CHUNK]
                xdt = xs_ref[rows, cols] * dt_f
                a_row_s = jnp.sum(jnp.where(ident, a_f, 0.0), axis=0, keepdims=True)
                dec = jnp.exp(jnp.where(causal, a_f - a_row_s, -jnp.inf))
                sc = (cb4 * dec).astype(BF16)
                xdt_b = xdt.astype(BF16)
                zero_b = jnp.zeros_like(xdt_b)
                bd = jnp.concatenate([jnp.where(lane_head == i, xdt_b, zero_b)
                                      for i in range(HEADS_PER_BLOCK)], axis=0)
                y_diag = jnp.dot(sc, bd, preferred_element_type=F32)
                a_last = a_f[CHUNK - 1:CHUNK, :]
                st = state_ref[:, cols]
                y_off = jnp.dot(c_g, st.astype(BF16), preferred_element_type=F32) * jnp.exp(a_f)
                y_ref[rows, cols] = y_diag + y_off
                wx = (xdt * jnp.exp(a_last - a_f)).astype(BF16)
                upd = lax.dot_general(b_g, wx, (((0,), (0,)), ((), ())), preferred_element_type=F32)
                state_ref[:, cols] = st * jnp.exp(a_last) + upd
        return carry

    lax.fori_loop(0, ts // CHUNK, chunk, 0)

    gw = SSD_INNER // SSD_GROUPS
    for g in range(SSD_GROUPS):
        cols = slice(g * gw, (g + 1) * gw)
        y = (y_ref[:, cols] + dskip_ref[:, cols] * xs_ref[:, cols]) * zs_ref[:, cols]
        r = lax.rsqrt(jnp.mean(y * y, axis=-1, keepdims=True) + EPS)
        yn_ref[:, cols] = (y * r * snw_ref[:, cols]).astype(BF16)
    br_a = jnp.dot(yn_ref[...], wssd_ref[...], preferred_element_type=F32)
    br_b = jnp.dot(sbv_ref[...], wsc_ref[...], preferred_element_type=F32)
    merged = (ga_ref[...] * br_a + gb_ref[...] * br_b).astype(BF16)
    out_ref[...] = x_ref[...] + jnp.dot(merged, wo_ref[...], preferred_element_type=F32)


def _ffn_kernel(h_ref, nw_ref, wg_ref, wu_ref, wd_ref, fw_ref, out_ref, *, n_split):
    h = h_ref[...]
    hn = (h * lax.rsqrt(jnp.mean(h * h, axis=-1, keepdims=True) + EPS) * nw_ref[...]).astype(BF16)
    hidden = wg_ref.shape[1]
    w = hidden // n_split
    acc = h
    for k in range(n_split):
        cols = slice(k * w, (k + 1) * w)
        g = jnp.dot(hn, wg_ref[:, cols], preferred_element_type=F32)
        u = jnp.dot(hn, wu_ref[:, cols], preferred_element_type=F32)
        act = (jax.nn.silu(g) * u).astype(BF16)
        acc = acc + jnp.dot(act, wd_ref[cols, :], preferred_element_type=F32)
    out_ref[...] = acc * lax.rsqrt(jnp.mean(acc * acc, axis=-1, keepdims=True) + EPS) * fw_ref[...]


def _const_spec(shape):
    return pl.BlockSpec(shape, lambda *_: (0,) * len(shape), pipeline_mode=pl.Buffered(1))


def _pad_lanes(v):
    return jnp.pad(v.astype(F32), (0, LANES - v.shape[0])).reshape(1, LANES)


def _mixer(x2, nw, w_in, cw, cb, dtb, alog, dskip, snw, w_ssd, scw, w_sc, w_o, batch, seq):
    ts = MIXER_TS
    nt = seq // ts
    o_dt = SSD_INNER + SSD_XBC
    w_in_r = jnp.concatenate(
        [w_in[:, :o_dt], w_in[:, o_dt + SSD_HEADS:], w_in[:, o_dt:o_dt + SSD_HEADS],
         jnp.zeros((D_MODEL, LANES - SSD_HEADS), w_in.dtype)], axis=1).astype(BF16)
    head_of_lane = jnp.arange(SSD_INNER) // SSD_HEAD_DIM
    e1 = (jnp.arange(LANES)[:, None] == head_of_lane[None, :]).astype(BF16)
    e2 = jnp.concatenate([e1, e1], axis=0)
    dskip_f = jnp.repeat(dskip.astype(F32), SSD_HEAD_DIM).reshape(1, SSD_INNER)

    args = (x2, nw.reshape(1, D_MODEL), w_in_r, cw, cb.reshape(1, SSD_XBC), _pad_lanes(dtb),
            _pad_lanes(alog), dskip_f, snw.reshape(1, SSD_INNER), w_ssd.astype(BF16), scw,
            w_sc.astype(BF16), w_o.astype(BF16), e2)
    in_specs = [pl.BlockSpec((ts, D_MODEL), lambda b, j: (b * nt + j, 0))]
    in_specs += [_const_spec(a.shape) for a in args[1:]]
    scratch = [
        pltpu.VMEM((CARRY + ts, SSD_XBC), F32),
        pltpu.VMEM((CARRY + ts, D_MODEL), F32),
        pltpu.VMEM((ts, SSD_INNER), F32),
        pltpu.VMEM((ts, SSD_INNER), F32),
        pltpu.VMEM((ts, SSD_GROUPS * SSD_STATE), BF16),
        pltpu.VMEM((ts, SSD_GROUPS * SSD_STATE), BF16),
        pltpu.VMEM((ts, LANES), F32),
        pltpu.VMEM((ts, SSD_INNER), F32),
        pltpu.VMEM((ts, SSD_INNER), BF16),
        pltpu.VMEM((ts, D_MODEL), BF16),
        pltpu.VMEM((ts, D_MODEL), F32),
        pltpu.VMEM((ts, D_MODEL), F32),
        pltpu.VMEM((SSD_STATE, SSD_INNER), F32),
    ]
    return pl.pallas_call(
        functools.partial(_mixer_kernel, ts=ts),
        grid=(batch, nt),
        in_specs=in_specs,
        out_specs=pl.BlockSpec((ts, D_MODEL), lambda b, j: (b * nt + j, 0)),
        out_shape=jax.ShapeDtypeStruct(x2.shape, F32),
        scratch_shapes=scratch,
        compiler_params=pltpu.CompilerParams(
            dimension_semantics=("arbitrary", "arbitrary"), vmem_limit_bytes=VMEM_LIMIT),
        name="mixer",
    )(*args)


def _ffn(h, nw, w_gate, w_up, w_down, fw):
    tm = FFN_TM
    t = h.shape[0]
    args = (h, nw.reshape(1, D_MODEL), w_gate.astype(BF16), w_up.astype(BF16), w_down.astype(BF16),
            fw.reshape(1, D_MODEL))
    in_specs = [pl.BlockSpec((tm, D_MODEL), lambda i: (i, 0))]
    in_specs += [_const_spec(a.shape) for a in args[1:]]
    return pl.pallas_call(
        functools.partial(_ffn_kernel, n_split=2),
        grid=(t // tm,),
        in_specs=in_specs,
        out_specs=pl.BlockSpec((tm, D_MODEL), lambda i: (i, 0)),
        out_shape=jax.ShapeDtypeStruct(h.shape, F32),
        compiler_params=pltpu.CompilerParams(
            dimension_semantics=("arbitrary",), vmem_limit_bytes=VMEM_LIMIT),
        name="ffn",
    )(*args)


def kernel(x, norm_mix_w, w_in, ssd_conv_w, ssd_conv_b, dt_bias, a_log, d_skip, ssd_norm_w, w_ssd_proj,
           sconv_w, w_sconv_proj, w_o, norm_ffn_w, w_gate, w_up, w_down, final_norm_w):
    batch, seq, d = x.shape
    assert w_in.shape[0] == 1
    h = x.reshape(batch * seq, d)
    h = _mixer(h, norm_mix_w[0], w_in[0], ssd_conv_w[0], ssd_conv_b[0], dt_bias[0], a_log[0],
               d_skip[0], ssd_norm_w[0], w_ssd_proj[0], sconv_w[0], w_sconv_proj[0], w_o[0],
               batch, seq)
    h = _ffn(h, norm_ffn_w[0], w_gate[0], w_up[0], w_down[0], final_norm_w)
    return h.reshape(batch, seq, d)
```

```python
import functools

import jax
import jax.numpy as jnp
from jax import lax
from jax.experimental import pallas as pl
from jax.experimental.pallas import tpu as pltpu

F32 = jnp.float32
BF16 = jnp.bfloat16

EPS = 1e-6
D_MODEL = 1024
SSD_INNER = 2048
SSD_HEAD_DIM = 64
SSD_HEADS = 32
SSD_GROUPS = 4
SSD_STATE = 128
SSD_CONV = 4
SSD_XBC = SSD_INNER + 2 * SSD_GROUPS * SSD_STATE
SCONV_K = 3
CHUNK = 64
LANES = 128
SUBLANES = 8
HEADS_PER_BLOCK = 4
BLOCK = HEADS_PER_BLOCK * SSD_HEAD_DIM
N_BLOCKS = SSD_INNER // BLOCK

O_Z = 0
O_XBC = O_Z + SSD_INNER
O_SB = O_XBC + SSD_XBC
O_SC = O_SB + D_MODEL
O_SX = O_SC + D_MODEL
O_GA = O_SX + D_MODEL
O_GB = O_GA + D_MODEL
O_DT = O_GB + D_MODEL
IN_DIM_PAD = O_DT + LANES

MIXER_TS = 256
FFN_TM = 512
VMEM_LIMIT = 58 * 1024 * 1024


def _softplus(v):
    return jnp.maximum(v, 0.0) + jnp.log(1.0 + jnp.exp(-jnp.abs(v)))


def _split2(v):
    hi = v.astype(BF16)
    lo = (v - hi.astype(F32)).astype(BF16)
    return jnp.concatenate([hi, lo], axis=1)


def _shift_rows(s, carry_row):
    rolled = pltpu.roll(s, 1, axis=0)
    head = rolled[0:SUBLANES]
    row = lax.broadcasted_iota(jnp.int32, head.shape, 0)
    head = jnp.where(row == 0, carry_row, head)
    return jnp.concatenate([head, rolled[SUBLANES:]], axis=0)


def _causal_conv(x, w_ref, cols, carry_ref, taps):
    ts = x.shape[0]
    s = w_ref[0:1, cols] * x
    for i in range(1, taps):
        carry = carry_ref[i - 1:i, cols]
        carry_ref[i - 1:i, cols] = s[ts - 1:ts]
        s = w_ref[i:i + 1, cols] * x + _shift_rows(s, carry)
    return s


def _mixer_kernel(x_ref, nw_ref, win_ref, cw_ref, cb_ref, dtb_ref, alog_ref, dskip_ref, snw_ref,
                  wssd_ref, scw_ref, wsc_ref, wo_ref, e2_ref, tril_ref, out_ref,
                  xbc_ref, cc_ref, pc_ref, zs_ref, xs_ref, b_ref, c_ref, af_ref, xdt_ref, y_ref,
                  yn_ref, sbv_ref, ga_ref, gb_ref, state_ref, *, ts):
    j = pl.program_id(1)

    @pl.when(j == 0)
    def _():
        cc_ref[...] = jnp.zeros_like(cc_ref)
        pc_ref[...] = jnp.zeros_like(pc_ref)
        state_ref[...] = jnp.zeros_like(state_ref)

    x = x_ref[...]
    xn = (x * lax.rsqrt(jnp.mean(x * x, axis=-1, keepdims=True) + EPS) * nw_ref[...]).astype(BF16)

    def proj(lo, hi):
        return jnp.dot(xn, win_ref[:, lo:hi], preferred_element_type=F32)

    for k in range(3):
        xbc_ref[:, k * 1024:(k + 1) * 1024] = proj(O_XBC + k * 1024, O_XBC + (k + 1) * 1024)
    cw = 256
    for k in range(SSD_XBC // cw):
        cols = slice(k * cw, (k + 1) * cw)
        act = jax.nn.silu(_causal_conv(xbc_ref[:, cols], cw_ref, cols, cc_ref, SSD_CONV) + cb_ref[:, cols])
        if k < SSD_INNER // cw:
            xs_ref[:, cols] = act
        elif k < (SSD_INNER + SSD_GROUPS * SSD_STATE) // cw:
            o = k * cw - SSD_INNER
            b_ref[:, o:o + cw] = act.astype(BF16)
        else:
            o = k * cw - SSD_INNER - SSD_GROUPS * SSD_STATE
            c_ref[:, o:o + cw] = act.astype(BF16)

    for k in range(D_MODEL // cw):
        cols = slice(k * cw, (k + 1) * cw)
        p = proj(O_SC + k * cw, O_SC + (k + 1) * cw) * proj(O_SX + k * cw, O_SX + (k + 1) * cw)
        v = _causal_conv(p, scw_ref, cols, pc_ref, SCONV_K)
        sbv_ref[:, cols] = (proj(O_SB + k * cw, O_SB + (k + 1) * cw) * v).astype(BF16)

    dt = _softplus(proj(O_DT, O_DT + LANES) + dtb_ref[...])
    da = dt * (-jnp.exp(alog_ref[...]))
    hi = da.astype(BF16)
    r1 = da - hi.astype(F32)
    mid = r1.astype(BF16)
    lo = (r1 - mid.astype(F32)).astype(BF16)
    cs = jnp.dot(tril_ref[...], jnp.concatenate([hi, mid, lo], axis=1), preferred_element_type=F32)
    acum = cs[:, 0:LANES] + cs[:, LANES:2 * LANES] + cs[:, 2 * LANES:3 * LANES]
    lhs_a = _split2(acum)
    lhs_d = _split2(dt)
    for blk in range(N_BLOCKS):
        cols = slice(blk * BLOCK, (blk + 1) * BLOCK)
        af_ref[:, cols] = jnp.dot(lhs_a, e2_ref[:, cols], preferred_element_type=F32)
        xdt_ref[:, cols] = xs_ref[:, cols] * jnp.dot(lhs_d, e2_ref[:, cols], preferred_element_type=F32)

    rowb = lax.broadcasted_iota(jnp.int32, (CHUNK, BLOCK), 0)
    laneb = lax.broadcasted_iota(jnp.int32, (CHUNK, BLOCK), 1)
    pos = laneb % SSD_HEAD_DIM
    causal = rowb >= pos
    ident = rowb == pos
    lane1 = lax.broadcasted_iota(jnp.int32, (CHUNK, LANES), 1)
    half_masks = (lane1 < SSD_HEAD_DIM, lane1 >= SSD_HEAD_DIM)
    zero_half = jnp.zeros((CHUNK, LANES), BF16)

    n_chunks = ts // CHUNK
    zw = SSD_INNER // n_chunks
    gwid = D_MODEL // n_chunks
    for c in range(n_chunks):
        zs_ref[:, c * zw:(c + 1) * zw] = jax.nn.silu(proj(O_Z + c * zw, O_Z + (c + 1) * zw))
        ga_ref[:, c * gwid:(c + 1) * gwid] = jax.nn.sigmoid(proj(O_GA + c * gwid, O_GA + (c + 1) * gwid))
        gb_ref[:, c * gwid:(c + 1) * gwid] = jax.nn.sigmoid(proj(O_GB + c * gwid, O_GB + (c + 1) * gwid))

        rows = slice(c * CHUNK, (c + 1) * CHUNK)
        for g in range(SSD_GROUPS):
            gcols = slice(g * SSD_STATE, (g + 1) * SSD_STATE)
            c_g = c_ref[rows, gcols]
            b_g = b_ref[rows, gcols]
            cb2 = lax.dot_general(c_g, jnp.concatenate([b_g, b_g], axis=0),
                                  (((1,), (1,)), ((), ())), preferred_element_type=F32)
            cb4 = jnp.concatenate([cb2, cb2], axis=1)
            for half in range(2):
                blk = g * 2 + half
                cols = slice(blk * BLOCK, (blk + 1) * BLOCK)
                a_f = af_ref[rows, cols]
                a_s = jnp.sum(jnp.where(ident, a_f, 0.0), axis=0, keepdims=True)
                dec = jnp.exp(jnp.where(causal, a_f - a_s, -jnp.inf))
                sc = (cb4 * dec).astype(BF16)
                xdt = xdt_ref[rows, cols]
                xdt_b = xdt.astype(BF16)
                blocks = []
                for i in range(HEADS_PER_BLOCK):
                    live = jnp.where(half_masks[i % 2], xdt_b[:, (i // 2) * LANES:(i // 2 + 1) * LANES], zero_half)
                    blocks.append(jnp.concatenate([live, zero_half] if i < 2 else [zero_half, live], axis=1))
                bd = jnp.concatenate(blocks, axis=0)
                y_diag = jnp.dot(sc, bd, preferred_element_type=F32)
                a_last = a_f[CHUNK - 1:CHUNK, :]
                st = state_ref[:, cols]
                y_off = jnp.dot(c_g, st.astype(BF16), preferred_element_type=F32) * jnp.exp(a_f)
                y_ref[rows, cols] = y_diag + y_off
                wx = (xdt * jnp.exp(a_last - a_f)).astype(BF16)
                upd = lax.dot_general(b_g, wx, (((0,), (0,)), ((), ())), preferred_element_type=F32)
                state_ref[:, cols] = st * jnp.exp(a_last) + upd

    gw = SSD_INNER // SSD_GROUPS
    for g in range(SSD_GROUPS):
        cols = slice(g * gw, (g + 1) * gw)
        y = (y_ref[:, cols] + dskip_ref[:, cols] * xs_ref[:, cols]) * zs_ref[:, cols]
        r = lax.rsqrt(jnp.mean(y * y, axis=-1, keepdims=True) + EPS)
        yn_ref[:, cols] = (y * r * snw_ref[:, cols]).astype(BF16)
    br_a = jnp.dot(yn_ref[...], wssd_ref[...], preferred_element_type=F32)
    br_b = jnp.dot(sbv_ref[...], wsc_ref[...], preferred_element_type=F32)
    merged = (ga_ref[...] * br_a + gb_ref[...] * br_b).astype(BF16)
    out_ref[...] = x_ref[...] + jnp.dot(merged, wo_ref[...], preferred_element_type=F32)


def _ffn_kernel(h_ref, nw_ref, wg_ref, wu_ref, wd_ref, fw_ref, out_ref, *, n_split):
    h = h_ref[...]
    hn = (h * lax.rsqrt(jnp.mean(h * h, axis=-1, keepdims=True) + EPS) * nw_ref[...]).astype(BF16)
    hidden = wg_ref.shape[1]
    w = hidden // n_split
    acc = h
    for k in range(n_split):
        cols = slice(k * w, (k + 1) * w)
        g = jnp.dot(hn, wg_ref[:, cols], preferred_element_type=F32)
        u = jnp.dot(hn, wu_ref[:, cols], preferred_element_type=F32)
        act = (jax.nn.silu(g) * u).astype(BF16)
        acc = acc + jnp.dot(act, wd_ref[cols, :], preferred_element_type=F32)
    out_ref[...] = acc * lax.rsqrt(jnp.mean(acc * acc, axis=-1, keepdims=True) + EPS) * fw_ref[...]


def _const_spec(shape):
    return pl.BlockSpec(shape, lambda *_: (0,) * len(shape), pipeline_mode=pl.Buffered(1))


def _pad_lanes(v):
    return jnp.pad(v.astype(F32), (0, LANES - v.shape[0])).reshape(1, LANES)


def _mixer(x2, nw, w_in, cw, cb, dtb, alog, dskip, snw, w_ssd, scw, w_sc, w_o, batch, seq):
    ts = MIXER_TS
    nt = seq // ts
    o_dt = SSD_INNER + SSD_XBC
    w_in_r = jnp.concatenate(
        [w_in[:, :o_dt], w_in[:, o_dt + SSD_HEADS:], w_in[:, o_dt:o_dt + SSD_HEADS],
         jnp.zeros((D_MODEL, LANES - SSD_HEADS), w_in.dtype)], axis=1).astype(BF16)
    head_of_lane = jnp.arange(SSD_INNER) // SSD_HEAD_DIM
    e1 = (jnp.arange(LANES)[:, None] == head_of_lane[None, :]).astype(BF16)
    e2 = jnp.concatenate([e1, e1], axis=0)
    r = jnp.arange(ts)
    tril = ((r[:, None] >= r[None, :]) & (r[:, None] // CHUNK == r[None, :] // CHUNK)).astype(BF16)
    dskip_f = jnp.repeat(dskip.astype(F32), SSD_HEAD_DIM).reshape(1, SSD_INNER)

    args = (x2, nw.reshape(1, D_MODEL), w_in_r, cw, cb.reshape(1, SSD_XBC), _pad_lanes(dtb),
            _pad_lanes(alog), dskip_f, snw.reshape(1, SSD_INNER), w_ssd.astype(BF16), scw,
            w_sc.astype(BF16), w_o.astype(BF16), e2, tril)
    in_specs = [pl.BlockSpec((ts, D_MODEL), lambda b, j: (b * nt + j, 0))]
    in_specs += [_const_spec(a.shape) for a in args[1:]]
    scratch = [
        pltpu.VMEM((ts, SSD_XBC), F32),
        pltpu.VMEM((SUBLANES, SSD_XBC), F32),
        pltpu.VMEM((SUBLANES, D_MODEL), F32),
        pltpu.VMEM((ts, SSD_INNER), F32),
        pltpu.VMEM((ts, SSD_INNER), F32),
        pltpu.VMEM((ts, SSD_GROUPS * SSD_STATE), BF16),
        pltpu.VMEM((ts, SSD_GROUPS * SSD_STATE), BF16),
        pltpu.VMEM((ts, SSD_INNER), F32),
        pltpu.VMEM((ts, SSD_INNER), F32),
        pltpu.VMEM((ts, SSD_INNER), F32),
        pltpu.VMEM((ts, SSD_INNER), BF16),
        pltpu.VMEM((ts, D_MODEL), BF16),
        pltpu.VMEM((ts, D_MODEL), F32),
        pltpu.VMEM((ts, D_MODEL), F32),
        pltpu.VMEM((SSD_STATE, SSD_INNER), F32),
    ]
    return pl.pallas_call(
        functools.partial(_mixer_kernel, ts=ts),
        grid=(batch, nt),
        in_specs=in_specs,
        out_specs=pl.BlockSpec((ts, D_MODEL), lambda b, j: (b * nt + j, 0)),
        out_shape=jax.ShapeDtypeStruct(x2.shape, F32),
        scratch_shapes=scratch,
        compiler_params=pltpu.CompilerParams(
            dimension_semantics=("arbitrary", "arbitrary"), vmem_limit_bytes=VMEM_LIMIT),
        name="mixer",
    )(*args)


def _ffn(h, nw, w_gate, w_up, w_down, fw):
    tm = FFN_TM
    t = h.shape[0]
    args = (h, nw.reshape(1, D_MODEL), w_gate.astype(BF16), w_up.astype(BF16), w_down.astype(BF16),
            fw.reshape(1, D_MODEL))
    in_specs = [pl.BlockSpec((tm, D_MODEL), lambda i: (i, 0))]
    in_specs += [_const_spec(a.shape) for a in args[1:]]
    return pl.pallas_call(
        functools.partial(_ffn_kernel, n_split=2),
        grid=(t // tm,),
        in_specs=in_specs,
        out_specs=pl.BlockSpec((tm, D_MODEL), lambda i: (i, 0)),
        out_shape=jax.ShapeDtypeStruct(h.shape, F32),
        compiler_params=pltpu.CompilerParams(
            dimension_semantics=("arbitrary",), vmem_limit_bytes=VMEM_LIMIT),
        name="ffn",
    )(*args)


def kernel(x, norm_mix_w, w_in, ssd_conv_w, ssd_conv_b, dt_bias, a_log, d_skip, ssd_norm_w, w_ssd_proj,
           sconv_w, w_sconv_proj, w_o, norm_ffn_w, w_gate, w_up, w_down, final_norm_w):
    batch, seq, d = x.shape
    assert w_in.shape[0] == 1
    h = x.reshape(batch * seq, d)
    h = _mixer(h, norm_mix_w[0], w_in[0], ssd_conv_w[0], ssd_conv_b[0], dt_bias[0], a_log[0],
               d_skip[0], ssd_norm_w[0], w_ssd_proj[0], sconv_w[0], w_sconv_proj[0], w_o[0],
               batch, seq)
    h = _ffn(h, norm_ffn_w[0], w_gate[0], w_up[0], w_down[0], final_norm_w)
    return h.reshape(batch, seq, d)
```

```python
import functools

import jax
import jax.numpy as jnp
from jax import lax
from jax.experimental import pallas as pl
from jax.experimental.pallas import tpu as pltpu

F32 = jnp.float32
BF16 = jnp.bfloat16

EPS = 1e-6
LOG2E = 1.4426950408889634
D_MODEL = 1024
SSD_INNER = 2048
SSD_HEAD_DIM = 64
SSD_HEADS = 32
SSD_GROUPS = 4
SSD_STATE = 128
SSD_CONV = 4
SSD_XBC = SSD_INNER + 2 * SSD_GROUPS * SSD_STATE
SCONV_K = 3
CHUNK = 64
LANES = 128
SUBLANES = 8
HEADS_PER_BLOCK = 4
BLOCK = HEADS_PER_BLOCK * SSD_HEAD_DIM
N_BLOCKS = SSD_INNER // BLOCK

O_Z = 0
O_XBC = O_Z + SSD_INNER
O_SB = O_XBC + SSD_XBC
O_SC = O_SB + D_MODEL
O_SX = O_SC + D_MODEL
O_GA = O_SX + D_MODEL
O_GB = O_GA + D_MODEL
O_DT = O_GB + D_MODEL

MIXER_TS = 256
FFN_TM = 1024
MXU_TILE = 256
FFN_PIECE = 4 * MXU_TILE
VMEM_LIMIT = 58 * 1024 * 1024


def _softplus(v):
    return jnp.maximum(v, 0.0) + jnp.log(1.0 + jnp.exp(-jnp.abs(v)))


def _split2(v):
    hi = v.astype(BF16)
    lo = (v - hi.astype(F32)).astype(BF16)
    return jnp.concatenate([hi, lo], axis=1)


def _shift_rows(s, carry_row):
    rolled = pltpu.roll(s, 1, axis=0)
    head = rolled[0:SUBLANES]
    row = lax.broadcasted_iota(jnp.int32, head.shape, 0)
    head = jnp.where(row == 0, carry_row, head)
    return jnp.concatenate([head, rolled[SUBLANES:]], axis=0)


def _causal_conv(x, w_ref, cols, carry_ref, taps):
    ts = x.shape[0]
    s = w_ref[0:1, cols] * x
    for i in range(1, taps):
        carry = carry_ref[i - 1:i, cols]
        carry_ref[i - 1:i, cols] = s[ts - 1:ts]
        s = w_ref[i:i + 1, cols] * x + _shift_rows(s, carry)
    return s


def _mixer_kernel(x_ref, nw_ref, win_ref, cw_ref, cb_ref, dtb_ref, alog_ref, dskip_ref, snw_ref,
                  wssd_ref, scw_ref, wsc_ref, wo_ref, e2_ref, tril_ref, out_ref,
                  xbc_ref, cc_ref, pc_ref, zs_ref, xs_ref, b_ref, c_ref, af_ref, xdt_ref, y_ref,
                  yn_ref, sbv_ref, ga_ref, gb_ref, state_ref, *, ts):
    j = pl.program_id(1)

    @pl.when(j == 0)
    def _():
        cc_ref[...] = jnp.zeros_like(cc_ref)
        pc_ref[...] = jnp.zeros_like(pc_ref)
        state_ref[...] = jnp.zeros_like(state_ref)

    x = x_ref[...]
    xn = (x * lax.rsqrt(jnp.mean(x * x, axis=-1, keepdims=True) + EPS) * nw_ref[...]).astype(BF16)

    def proj(lo, hi):
        return jnp.dot(xn, win_ref[:, lo:hi], preferred_element_type=F32)

    for k in range(3):
        xbc_ref[:, k * 1024:(k + 1) * 1024] = proj(O_XBC + k * 1024, O_XBC + (k + 1) * 1024)
    cw = 256
    for k in range(SSD_XBC // cw):
        cols = slice(k * cw, (k + 1) * cw)
        act = jax.nn.silu(_causal_conv(xbc_ref[:, cols], cw_ref, cols, cc_ref, SSD_CONV) + cb_ref[:, cols])
        if k < SSD_INNER // cw:
            xs_ref[:, cols] = act
        elif k < (SSD_INNER + SSD_GROUPS * SSD_STATE) // cw:
            o = k * cw - SSD_INNER
            b_ref[:, o:o + cw] = act.astype(BF16)
        else:
            o = k * cw - SSD_INNER - SSD_GROUPS * SSD_STATE
            c_ref[:, o:o + cw] = act.astype(BF16)

    for k in range(D_MODEL // cw):
        cols = slice(k * cw, (k + 1) * cw)
        p = proj(O_SC + k * cw, O_SC + (k + 1) * cw) * proj(O_SX + k * cw, O_SX + (k + 1) * cw)
        v = _causal_conv(p, scw_ref, cols, pc_ref, SCONV_K)
        sbv_ref[:, cols] = (proj(O_SB + k * cw, O_SB + (k + 1) * cw) * v).astype(BF16)

    dt = _softplus(proj(O_DT, O_DT + LANES) + dtb_ref[...])
    da = dt * (-jnp.exp(alog_ref[...]))
    hi = da.astype(BF16)
    r1 = da - hi.astype(F32)
    mid = r1.astype(BF16)
    lo = (r1 - mid.astype(F32)).astype(BF16)
    cs = jnp.dot(tril_ref[...], jnp.concatenate([hi, mid, lo], axis=1), preferred_element_type=F32)
    acum = (cs[:, 0:LANES] + cs[:, LANES:2 * LANES] + cs[:, 2 * LANES:3 * LANES]) * LOG2E
    lhs_a = _split2(acum)
    lhs_d = _split2(dt)
    for blk in range(N_BLOCKS):
        cols = slice(blk * BLOCK, (blk + 1) * BLOCK)
        af_ref[:, cols] = jnp.dot(lhs_a, e2_ref[:, cols], preferred_element_type=F32)
        xdt_ref[:, cols] = xs_ref[:, cols] * jnp.dot(lhs_d, e2_ref[:, cols], preferred_element_type=F32)

    rowb = lax.broadcasted_iota(jnp.int32, (CHUNK, BLOCK), 0)
    laneb = lax.broadcasted_iota(jnp.int32, (CHUNK, BLOCK), 1)
    pos = laneb % SSD_HEAD_DIM
    causal = rowb >= pos
    ident = rowb == pos
    lane1 = lax.broadcasted_iota(jnp.int32, (CHUNK, LANES), 1)
    half_masks = (lane1 < SSD_HEAD_DIM, lane1 >= SSD_HEAD_DIM)
    zero_half = jnp.zeros((CHUNK, LANES), BF16)

    n_chunks = ts // CHUNK
    zw = SSD_INNER // n_chunks
    gwid = D_MODEL // n_chunks
    for c in range(n_chunks):
        zs_ref[:, c * zw:(c + 1) * zw] = jax.nn.silu(proj(O_Z + c * zw, O_Z + (c + 1) * zw))
        ga_ref[:, c * gwid:(c + 1) * gwid] = jax.nn.sigmoid(proj(O_GA + c * gwid, O_GA + (c + 1) * gwid))
        gb_ref[:, c * gwid:(c + 1) * gwid] = jax.nn.sigmoid(proj(O_GB + c * gwid, O_GB + (c + 1) * gwid))

        rows = slice(c * CHUNK, (c + 1) * CHUNK)
        for g in range(SSD_GROUPS):
            gcols = slice(g * SSD_STATE, (g + 1) * SSD_STATE)
            c_g = c_ref[rows, gcols]
            b_g = b_ref[rows, gcols]
            cb2 = lax.dot_general(c_g, jnp.concatenate([b_g, b_g], axis=0),
                                  (((1,), (1,)), ((), ())), preferred_element_type=F32)
            cb4 = jnp.concatenate([cb2, cb2], axis=1)
            for half in range(2):
                blk = g * 2 + half
                cols = slice(blk * BLOCK, (blk + 1) * BLOCK)
                a_f = af_ref[rows, cols]
                a_s = jnp.sum(jnp.where(ident, a_f, 0.0), axis=0, keepdims=True)
                dec = jnp.exp2(jnp.where(causal, a_f - a_s, -jnp.inf))
                sc = (cb4 * dec).astype(BF16)
                xdt = xdt_ref[rows, cols]
                xdt_b = xdt.astype(BF16)
                blocks = []
                for i in range(HEADS_PER_BLOCK):
                    live = jnp.where(half_masks[i % 2], xdt_b[:, (i // 2) * LANES:(i // 2 + 1) * LANES], zero_half)
                    blocks.append(jnp.concatenate([live, zero_half] if i < 2 else [zero_half, live], axis=1))
                bd = jnp.concatenate(blocks, axis=0)
                y_diag = jnp.dot(sc, bd, preferred_element_type=F32)
                a_last = a_f[CHUNK - 1:CHUNK, :]
                st = state_ref[:, cols]
                y_off = jnp.dot(c_g, st.astype(BF16), preferred_element_type=F32) * jnp.exp2(a_f)
                y_ref[rows, cols] = y_diag + y_off
                wx = (xdt * jnp.exp2(a_last - a_f)).astype(BF16)
                upd = lax.dot_general(b_g, wx, (((0,), (0,)), ((), ())), preferred_element_type=F32)
                state_ref[:, cols] = st * jnp.exp2(a_last) + upd

    gw = SSD_INNER // SSD_GROUPS
    for g in range(SSD_GROUPS):
        cols = slice(g * gw, (g + 1) * gw)
        y = (y_ref[:, cols] + dskip_ref[:, cols] * xs_ref[:, cols]) * zs_ref[:, cols]
        r = lax.rsqrt(jnp.mean(y * y, axis=-1, keepdims=True) + EPS)
        yn_ref[:, cols] = (y * r * snw_ref[:, cols]).astype(BF16)
    br_a = jnp.dot(yn_ref[...], wssd_ref[...], preferred_element_type=F32)
    br_b = jnp.dot(sbv_ref[...], wsc_ref[...], preferred_element_type=F32)
    merged = (ga_ref[...] * br_a + gb_ref[...] * br_b).astype(BF16)
    out_ref[...] = x_ref[...] + jnp.dot(merged, wo_ref[...], preferred_element_type=F32)


def _ffn_kernel(h_ref, nw_ref, wg_ref, wu_ref, wd_ref, fw_ref, out_ref, *, piece):
    h = h_ref[...]
    hn = (h * lax.rsqrt(jnp.mean(h * h, axis=-1, keepdims=True) + EPS) * nw_ref[...]).astype(BF16)
    hidden = wg_ref.shape[1]
    acc = h
    for lo in range(0, hidden, piece):
        cols = slice(lo, min(lo + piece, hidden))
        g = jnp.dot(hn, wg_ref[:, cols], preferred_element_type=F32)
        u = jnp.dot(hn, wu_ref[:, cols], preferred_element_type=F32)
        act = (jax.nn.silu(g) * u).astype(BF16)
        acc = acc + jnp.dot(act, wd_ref[cols, :], preferred_element_type=F32)
    out_ref[...] = acc * lax.rsqrt(jnp.mean(acc * acc, axis=-1, keepdims=True) + EPS) * fw_ref[...]


def _const_spec(shape):
    return pl.BlockSpec(shape, lambda *_: (0,) * len(shape), pipeline_mode=pl.Buffered(1))


def _pad_lanes(v):
    return jnp.pad(v.astype(F32), (0, LANES - v.shape[0])).reshape(1, LANES)


def _mixer(x2, nw, w_in, cw, cb, dtb, alog, dskip, snw, w_ssd, scw, w_sc, w_o, batch, seq):
    ts = MIXER_TS
    nt = seq // ts
    o_dt = SSD_INNER + SSD_XBC
    w_in_r = jnp.concatenate(
        [w_in[:, :o_dt], w_in[:, o_dt + SSD_HEADS:], w_in[:, o_dt:o_dt + SSD_HEADS],
         jnp.zeros((D_MODEL, LANES - SSD_HEADS), w_in.dtype)], axis=1).astype(BF16)
    head_of_lane = jnp.arange(SSD_INNER) // SSD_HEAD_DIM
    e1 = (jnp.arange(LANES)[:, None] == head_of_lane[None, :]).astype(BF16)
    e2 = jnp.concatenate([e1, e1], axis=0)
    r = jnp.arange(ts)
    tril = ((r[:, None] >= r[None, :]) & (r[:, None] // CHUNK == r[None, :] // CHUNK)).astype(BF16)
    dskip_f = jnp.repeat(dskip.astype(F32), SSD_HEAD_DIM).reshape(1, SSD_INNER)

    args = (x2, nw.reshape(1, D_MODEL), w_in_r, cw, cb.reshape(1, SSD_XBC), _pad_lanes(dtb),
            _pad_lanes(alog), dskip_f, snw.reshape(1, SSD_INNER), w_ssd.astype(BF16), scw,
            w_sc.astype(BF16), w_o.astype(BF16), e2, tril)
    in_specs = [pl.BlockSpec((ts, D_MODEL), lambda b, j: (b * nt + j, 0))]
    in_specs += [_const_spec(a.shape) for a in args[1:]]
    scratch = [
        pltpu.VMEM((ts, SSD_XBC), F32),
        pltpu.VMEM((SUBLANES, SSD_XBC), F32),
        pltpu.VMEM((SUBLANES, D_MODEL), F32),
        pltpu.VMEM((ts, SSD_INNER), F32),
        pltpu.VMEM((ts, SSD_INNER), F32),
        pltpu.VMEM((ts, SSD_GROUPS * SSD_STATE), BF16),
        pltpu.VMEM((ts, SSD_GROUPS * SSD_STATE), BF16),
        pltpu.VMEM((ts, SSD_INNER), F32),
        pltpu.VMEM((ts, SSD_INNER), F32),
        pltpu.VMEM((ts, SSD_INNER), F32),
        pltpu.VMEM((ts, SSD_INNER), BF16),
        pltpu.VMEM((ts, D_MODEL), BF16),
        pltpu.VMEM((ts, D_MODEL), F32),
        pltpu.VMEM((ts, D_MODEL), F32),
        pltpu.VMEM((SSD_STATE, SSD_INNER), F32),
    ]
    return pl.pallas_call(
        functools.partial(_mixer_kernel, ts=ts),
        grid=(batch, nt),
        in_specs=in_specs,
        out_specs=pl.BlockSpec((ts, D_MODEL), lambda b, j: (b * nt + j, 0)),
        out_shape=jax.ShapeDtypeStruct(x2.shape, F32),
        scratch_shapes=scratch,
        compiler_params=pltpu.CompilerParams(
            dimension_semantics=("arbitrary", "arbitrary"), vmem_limit_bytes=VMEM_LIMIT),
        name="mixer",
    )(*args)


def _ffn(h, nw, w_gate, w_up, w_down, fw):
    tm = FFN_TM
    t = h.shape[0]
    args = (h, nw.reshape(1, D_MODEL), w_gate.astype(BF16), w_up.astype(BF16), w_down.astype(BF16),
            fw.reshape(1, D_MODEL))
    in_specs = [pl.BlockSpec((tm, D_MODEL), lambda i: (i, 0))]
    in_specs += [_const_spec(a.shape) for a in args[1:]]
    return pl.pallas_call(
        functools.partial(_ffn_kernel, piece=FFN_PIECE),
        grid=(t // tm,),
        in_specs=in_specs,
        out_specs=pl.BlockSpec((tm, D_MODEL), lambda i: (i, 0)),
        out_shape=jax.ShapeDtypeStruct(h.shape, F32),
        compiler_params=pltpu.CompilerParams(
            dimension_semantics=("arbitrary",), vmem_limit_bytes=VMEM_LIMIT),
        name="ffn",
    )(*args)


def kernel(x, norm_mix_w, w_in, ssd_conv_w, ssd_conv_b, dt_bias, a_log, d_skip, ssd_norm_w, w_ssd_proj,
           sconv_w, w_sconv_proj, w_o, norm_ffn_w, w_gate, w_up, w_down, final_norm_w):
    batch, seq, d = x.shape
    assert w_in.shape[0] == 1
    h = x.reshape(batch * seq, d)
    h = _mixer(h, norm_mix_w[0], w_in[0], ssd_conv_w[0], ssd_conv_b[0], dt_bias[0], a_log[0],
               d_skip[0], ssd_norm_w[0], w_ssd_proj[0], sconv_w[0], w_sconv_proj[0], w_o[0],
               batch, seq)
    h = _ffn(h, norm_ffn_w[0], w_gate[0], w_up[0], w_down[0], final_norm_w)
    return h.reshape(batch, seq, d)
```

```python
import functools

import jax
import jax.numpy as jnp
from jax import lax
from jax.experimental import pallas as pl
from jax.experimental.pallas import tpu as pltpu

F32 = jnp.float32
BF16 = jnp.bfloat16

EPS = 1e-6
LOG2E = 1.4426950408889634
D_MODEL = 1024
SSD_INNER = 2048
SSD_HEAD_DIM = 64
SSD_HEADS = 32
SSD_GROUPS = 4
SSD_STATE = 128
SSD_CONV = 4
SSD_XBC = SSD_INNER + 2 * SSD_GROUPS * SSD_STATE
SCONV_K = 3
CHUNK = 64
LANES = 128
SUBLANES = 8
HEADS_PER_BLOCK = 4
BLOCK = HEADS_PER_BLOCK * SSD_HEAD_DIM
N_BLOCKS = SSD_INNER // BLOCK

O_Z = 0
O_XBC = O_Z + SSD_INNER
O_SB = O_XBC + SSD_XBC
O_SC = O_SB + D_MODEL
O_SX = O_SC + D_MODEL
O_GA = O_SX + D_MODEL
O_GB = O_GA + D_MODEL
O_DT = O_GB + D_MODEL

MIXER_TS = 256
FFN_TM = 1024
MXU_TILE = 256
FFN_PIECE = 4 * MXU_TILE
VMEM_LIMIT = 58 * 1024 * 1024


def _softplus(v):
    return jnp.maximum(v, 0.0) + jnp.log(1.0 + jnp.exp(-jnp.abs(v)))


def _split2(v):
    hi = v.astype(BF16)
    lo = (v - hi.astype(F32)).astype(BF16)
    return jnp.concatenate([hi, lo], axis=1)


def _shift_rows(s, carry_row):
    rolled = pltpu.roll(s, 1, axis=0)
    head = rolled[0:SUBLANES]
    row = lax.broadcasted_iota(jnp.int32, head.shape, 0)
    head = jnp.where(row == 0, carry_row, head)
    return jnp.concatenate([head, rolled[SUBLANES:]], axis=0)


def _causal_conv(x, w_ref, cols, carry_ref, taps):
    ts = x.shape[0]
    s = w_ref[0:1, cols] * x
    for i in range(1, taps):
        carry = carry_ref[i - 1:i, cols]
        carry_ref[i - 1:i, cols] = s[ts - 1:ts]
        s = w_ref[i:i + 1, cols] * x + _shift_rows(s, carry)
    return s


def _mixer_kernel(x_ref, nw_ref, win_ref, cw_ref, cb_ref, dtb_ref, alog_ref, dskip_ref, snw_ref,
                  wssd_ref, scw_ref, wsc_ref, wo_ref, e2_ref, tril_ref, out_ref,
                  xbc_ref, cc_ref, pc_ref, zs_ref, xs_ref, b_ref, c_ref, af_ref, xdt_ref, y_ref,
                  yn_ref, sbv_ref, ga_ref, gb_ref, state_ref, *, ts):
    j = pl.program_id(1)

    @pl.when(j == 0)
    def _():
        cc_ref[...] = jnp.zeros_like(cc_ref)
        pc_ref[...] = jnp.zeros_like(pc_ref)
        state_ref[...] = jnp.zeros_like(state_ref)

    x = x_ref[...]
    xn = (x * lax.rsqrt(jnp.mean(x * x, axis=-1, keepdims=True) + EPS) * nw_ref[...]).astype(BF16)

    def proj(lo, hi):
        return jnp.dot(xn, win_ref[:, lo:hi], preferred_element_type=F32)

    cw = 256

    def xbc_proj(k):
        xbc_ref[:, k * cw:(k + 1) * cw] = proj(O_XBC + k * cw, O_XBC + (k + 1) * cw)

    xbc_proj(0)
    for k in range(SSD_XBC // cw):
        if k + 1 < SSD_XBC // cw:
            xbc_proj(k + 1)
        cols = slice(k * cw, (k + 1) * cw)
        act = jax.nn.silu(_causal_conv(xbc_ref[:, cols], cw_ref, cols, cc_ref, SSD_CONV) + cb_ref[:, cols])
        if k < SSD_INNER // cw:
            xs_ref[:, cols] = act
        elif k < (SSD_INNER + SSD_GROUPS * SSD_STATE) // cw:
            o = k * cw - SSD_INNER
            b_ref[:, o:o + cw] = act.astype(BF16)
        else:
            o = k * cw - SSD_INNER - SSD_GROUPS * SSD_STATE
            c_ref[:, o:o + cw] = act.astype(BF16)

    for k in range(D_MODEL // cw):
        cols = slice(k * cw, (k + 1) * cw)
        p = proj(O_SC + k * cw, O_SC + (k + 1) * cw) * proj(O_SX + k * cw, O_SX + (k + 1) * cw)
        v = _causal_conv(p, scw_ref, cols, pc_ref, SCONV_K)
        sbv_ref[:, cols] = (proj(O_SB + k * cw, O_SB + (k + 1) * cw) * v).astype(BF16)

    dt = _softplus(proj(O_DT, O_DT + LANES) + dtb_ref[...])
    da = dt * (-jnp.exp(alog_ref[...]))
    hi = da.astype(BF16)
    r1 = da - hi.astype(F32)
    mid = r1.astype(BF16)
    lo = (r1 - mid.astype(F32)).astype(BF16)
    cs = jnp.dot(tril_ref[...], jnp.concatenate([hi, mid, lo], axis=1), preferred_element_type=F32)
    acum = (cs[:, 0:LANES] + cs[:, LANES:2 * LANES] + cs[:, 2 * LANES:3 * LANES]) * LOG2E
    lhs_a = _split2(acum)
    lhs_d = _split2(dt)
    for blk in range(N_BLOCKS):
        cols = slice(blk * BLOCK, (blk + 1) * BLOCK)
        af_ref[:, cols] = jnp.dot(lhs_a, e2_ref[:, cols], preferred_element_type=F32)
        xdt_ref[:, cols] = xs_ref[:, cols] * jnp.dot(lhs_d, e2_ref[:, cols], preferred_element_type=F32)

    rowb = lax.broadcasted_iota(jnp.int32, (CHUNK, BLOCK), 0)
    laneb = lax.broadcasted_iota(jnp.int32, (CHUNK, BLOCK), 1)
    pos = laneb % SSD_HEAD_DIM
    causal = rowb >= pos
    ident = rowb == pos
    lane1 = lax.broadcasted_iota(jnp.int32, (CHUNK, LANES), 1)
    half_masks = (lane1 < SSD_HEAD_DIM, lane1 >= SSD_HEAD_DIM)
    zero_half = jnp.zeros((CHUNK, LANES), BF16)

    n_chunks = ts // CHUNK
    pw = 256
    pieces = ([(zs_ref, O_Z, k, jax.nn.silu) for k in range(SSD_INNER // pw)]
              + [(ga_ref, O_GA, k, jax.nn.sigmoid) for k in range(D_MODEL // pw)]
              + [(gb_ref, O_GB, k, jax.nn.sigmoid) for k in range(D_MODEL // pw)])
    assert len(pieces) == n_chunks * SSD_GROUPS
    for c in range(n_chunks):
        rows = slice(c * CHUNK, (c + 1) * CHUNK)
        for g in range(SSD_GROUPS):
            ref, base, k, fn = pieces[c * SSD_GROUPS + g]
            ref[:, k * pw:(k + 1) * pw] = fn(proj(base + k * pw, base + (k + 1) * pw))
            gcols = slice(g * SSD_STATE, (g + 1) * SSD_STATE)
            c_g = c_ref[rows, gcols]
            b_g = b_ref[rows, gcols]
            cb2 = lax.dot_general(c_g, jnp.concatenate([b_g, b_g], axis=0),
                                  (((1,), (1,)), ((), ())), preferred_element_type=F32)
            cb4 = jnp.concatenate([cb2, cb2], axis=1)
            for half in range(2):
                blk = g * 2 + half
                cols = slice(blk * BLOCK, (blk + 1) * BLOCK)
                a_f = af_ref[rows, cols]
                a_s = jnp.sum(jnp.where(ident, a_f, 0.0), axis=0, keepdims=True)
                dec = jnp.exp2(jnp.where(causal, a_f - a_s, -jnp.inf))
                sc = (cb4 * dec).astype(BF16)
                xdt = xdt_ref[rows, cols]
                xdt_b = xdt.astype(BF16)
                blocks = []
                for i in range(HEADS_PER_BLOCK):
                    live = jnp.where(half_masks[i % 2], xdt_b[:, (i // 2) * LANES:(i // 2 + 1) * LANES], zero_half)
                    blocks.append(jnp.concatenate([live, zero_half] if i < 2 else [zero_half, live], axis=1))
                bd = jnp.concatenate(blocks, axis=0)
                y_diag = jnp.dot(sc, bd, preferred_element_type=F32)
                a_last = a_f[CHUNK - 1:CHUNK, :]
                st = state_ref[:, cols]
                y_off = jnp.dot(c_g, st.astype(BF16), preferred_element_type=F32) * jnp.exp2(a_f)
                y_ref[rows, cols] = y_diag + y_off
                wx = (xdt * jnp.exp2(a_last - a_f)).astype(BF16)
                upd = lax.dot_general(b_g, wx, (((0,), (0,)), ((), ())), preferred_element_type=F32)
                state_ref[:, cols] = st * jnp.exp2(a_last) + upd

    gw = SSD_INNER // SSD_GROUPS
    for g in range(SSD_GROUPS):
        cols = slice(g * gw, (g + 1) * gw)
        y = (y_ref[:, cols] + dskip_ref[:, cols] * xs_ref[:, cols]) * zs_ref[:, cols]
        r = lax.rsqrt(jnp.mean(y * y, axis=-1, keepdims=True) + EPS)
        yn_ref[:, cols] = (y * r * snw_ref[:, cols]).astype(BF16)
    br_a = jnp.dot(yn_ref[...], wssd_ref[...], preferred_element_type=F32)
    br_b = jnp.dot(sbv_ref[...], wsc_ref[...], preferred_element_type=F32)
    merged = (ga_ref[...] * br_a + gb_ref[...] * br_b).astype(BF16)
    out_ref[...] = x_ref[...] + jnp.dot(merged, wo_ref[...], preferred_element_type=F32)


def _ffn_kernel(h_ref, nw_ref, wg_ref, wu_ref, wd_ref, fw_ref, out_ref, *, piece):
    h = h_ref[...]
    hn = (h * lax.rsqrt(jnp.mean(h * h, axis=-1, keepdims=True) + EPS) * nw_ref[...]).astype(BF16)
    hidden = wg_ref.shape[1]
    acc = h
    for lo in range(0, hidden, piece):
        cols = slice(lo, min(lo + piece, hidden))
        g = jnp.dot(hn, wg_ref[:, cols], preferred_element_type=F32)
        u = jnp.dot(hn, wu_ref[:, cols], preferred_element_type=F32)
        act = (jax.nn.silu(g) * u).astype(BF16)
        acc = acc + jnp.dot(act, wd_ref[cols, :], preferred_element_type=F32)
    out_ref[...] = acc * lax.rsqrt(jnp.mean(acc * acc, axis=-1, keepdims=True) + EPS) * fw_ref[...]


def _const_spec(shape):
    return pl.BlockSpec(shape, lambda *_: (0,) * len(shape), pipeline_mode=pl.Buffered(1))


def _pad_lanes(v):
    return jnp.pad(v.astype(F32), (0, LANES - v.shape[0])).reshape(1, LANES)


def _mixer(x2, nw, w_in, cw, cb, dtb, alog, dskip, snw, w_ssd, scw, w_sc, w_o, batch, seq):
    ts = MIXER_TS
    nt = seq // ts
    o_dt = SSD_INNER + SSD_XBC
    w_in_r = jnp.pad(w_in, ((0, 0), (0, LANES - SSD_HEADS))).astype(BF16)
    w_in_r = lax.dynamic_update_slice(w_in_r, w_in[:, o_dt + SSD_HEADS:].astype(BF16), (0, o_dt))
    w_in_r = lax.dynamic_update_slice(w_in_r, w_in[:, o_dt:o_dt + SSD_HEADS].astype(BF16), (0, O_DT))
    head_of_lane = jnp.arange(SSD_INNER) // SSD_HEAD_DIM
    e1 = (jnp.arange(LANES)[:, None] == head_of_lane[None, :]).astype(BF16)
    e2 = jnp.concatenate([e1, e1], axis=0)
    r = jnp.arange(ts)
    tril = ((r[:, None] >= r[None, :]) & (r[:, None] // CHUNK == r[None, :] // CHUNK)).astype(BF16)
    dskip_f = jnp.repeat(dskip.astype(F32), SSD_HEAD_DIM).reshape(1, SSD_INNER)

    args = (x2, nw.reshape(1, D_MODEL), w_in_r, cw, cb.reshape(1, SSD_XBC), _pad_lanes(dtb),
            _pad_lanes(alog), dskip_f, snw.reshape(1, SSD_INNER), w_ssd.astype(BF16), scw,
            w_sc.astype(BF16), w_o.astype(BF16), e2, tril)
    in_specs = [pl.BlockSpec((ts, D_MODEL), lambda b, j: (b * nt + j, 0))]
    in_specs += [_const_spec(a.shape) for a in args[1:]]
    scratch = [
        pltpu.VMEM((ts, SSD_XBC), F32),
        pltpu.VMEM((SUBLANES, SSD_XBC), F32),
        pltpu.VMEM((SUBLANES, D_MODEL), F32),
        pltpu.VMEM((ts, SSD_INNER), F32),
        pltpu.VMEM((ts, SSD_INNER), F32),
        pltpu.VMEM((ts, SSD_GROUPS * SSD_STATE), BF16),
        pltpu.VMEM((ts, SSD_GROUPS * SSD_STATE), BF16),
        pltpu.VMEM((ts, SSD_INNER), F32),
        pltpu.VMEM((ts, SSD_INNER), F32),
        pltpu.VMEM((ts, SSD_INNER), F32),
        pltpu.VMEM((ts, SSD_INNER), BF16),
        pltpu.VMEM((ts, D_MODEL), BF16),
        pltpu.VMEM((ts, D_MODEL), F32),
        pltpu.VMEM((ts, D_MODEL), F32),
        pltpu.VMEM((SSD_STATE, SSD_INNER), F32),
    ]
    return pl.pallas_call(
        functools.partial(_mixer_kernel, ts=ts),
        grid=(batch, nt),
        in_specs=in_specs,
        out_specs=pl.BlockSpec((ts, D_MODEL), lambda b, j: (b * nt + j, 0)),
        out_shape=jax.ShapeDtypeStruct(x2.shape, F32),
        scratch_shapes=scratch,
        compiler_params=pltpu.CompilerParams(
            dimension_semantics=("arbitrary", "arbitrary"), vmem_limit_bytes=VMEM_LIMIT),
        name="mixer",
    )(*args)


def _ffn(h, nw, w_gate, w_up, w_down, fw):
    tm = FFN_TM
    t = h.shape[0]
    args = (h, nw.reshape(1, D_MODEL), w_gate.astype(BF16), w_up.astype(BF16), w_down.astype(BF16),
            fw.reshape(1, D_MODEL))
    in_specs = [pl.BlockSpec((tm, D_MODEL), lambda i: (i, 0))]
    in_specs += [_const_spec(a.shape) for a in args[1:]]
    return pl.pallas_call(
        functools.partial(_ffn_kernel, piece=FFN_PIECE),
        grid=(t // tm,),
        in_specs=in_specs,
        out_specs=pl.BlockSpec((tm, D_MODEL), lambda i: (i, 0)),
        out_shape=jax.ShapeDtypeStruct(h.shape, F32),
        compiler_params=pltpu.CompilerParams(
            dimension_semantics=("arbitrary",), vmem_limit_bytes=VMEM_LIMIT),
        name="ffn",
    )(*args)


def kernel(x, norm_mix_w, w_in, ssd_conv_w, ssd_conv_b, dt_bias, a_log, d_skip, ssd_norm_w, w_ssd_proj,
           sconv_w, w_sconv_proj, w_o, norm_ffn_w, w_gate, w_up, w_down, final_norm_w):
    batch, seq, d = x.shape
    assert w_in.shape[0] == 1
    h = x.reshape(batch * seq, d)
    h = _mixer(h, norm_mix_w[0], w_in[0], ssd_conv_w[0], ssd_conv_b[0], dt_bias[0], a_log[0],
               d_skip[0], ssd_norm_w[0], w_ssd_proj[0], sconv_w[0], w_sconv_proj[0], w_o[0],
               batch, seq)
    h = _ffn(h, norm_ffn_w[0], w_gate[0], w_up[0], w_down[0], final_norm_w)
    return h.reshape(batch, seq, d)
```

```python
import functools

import jax
import jax.numpy as jnp
from jax import lax
from jax.experimental import pallas as pl
from jax.experimental.pallas import tpu as pltpu

F32 = jnp.float32
BF16 = jnp.bfloat16

EPS = 1e-6
LOG2E = 1.4426950408889634
D_MODEL = 1024
SSD_INNER = 2048
SSD_HEAD_DIM = 64
SSD_HEADS = 32
SSD_GROUPS = 4
SSD_STATE = 128
SSD_CONV = 4
SSD_XBC = SSD_INNER + 2 * SSD_GROUPS * SSD_STATE
SCONV_K = 3
CHUNK = 64
LANES = 128
SUBLANES = 8
HEADS_PER_BLOCK = 4
BLOCK = HEADS_PER_BLOCK * SSD_HEAD_DIM
N_BLOCKS = SSD_INNER // BLOCK

O_Z = 0
O_XBC = O_Z + SSD_INNER
O_SB = O_XBC + SSD_XBC
O_SC = O_SB + D_MODEL
O_SX = O_SC + D_MODEL
O_GA = O_SX + D_MODEL
O_GB = O_GA + D_MODEL
O_DT = O_GB + D_MODEL

MIXER_TS = 256
FFN_TM = 1024
MXU_TILE = 256
FFN_PIECE = 4 * MXU_TILE
VMEM_LIMIT = 58 * 1024 * 1024


def _softplus(v):
    return jnp.maximum(v, 0.0) + jnp.log(1.0 + jnp.exp(-jnp.abs(v)))


def _split2(v):
    hi = v.astype(BF16)
    lo = (v - hi.astype(F32)).astype(BF16)
    return jnp.concatenate([hi, lo], axis=1)


def _shift_rows(s, carry_row):
    rolled = pltpu.roll(s, 1, axis=0)
    head = rolled[0:SUBLANES]
    row = lax.broadcasted_iota(jnp.int32, head.shape, 0)
    head = jnp.where(row == 0, carry_row, head)
    return jnp.concatenate([head, rolled[SUBLANES:]], axis=0)


def _causal_conv(x, w_ref, cols, carry_ref, taps):
    ts = x.shape[0]
    s = w_ref[0:1, cols] * x
    for i in range(1, taps):
        carry = carry_ref[i - 1:i, cols]
        carry_ref[i - 1:i, cols] = s[ts - 1:ts]
        s = w_ref[i:i + 1, cols] * x + _shift_rows(s, carry)
    return s


def _mixer_kernel(x_ref, nw_ref, win_ref, cw_ref, cb_ref, dtb_ref, alog_ref, dskip_ref, snw_ref,
                  wssd_ref, scw_ref, wsc_ref, wo_ref, e2_ref, tril_ref, out_ref,
                  xbc_ref, cc_ref, pc_ref, zs_ref, xs_ref, b_ref, c_ref, af_ref, xdt_ref, y_ref,
                  yn_ref, sbv_ref, ga_ref, gb_ref, state_ref, *, ts):
    j = pl.program_id(1)

    @pl.when(j == 0)
    def _():
        cc_ref[...] = jnp.zeros_like(cc_ref)
        pc_ref[...] = jnp.zeros_like(pc_ref)
        state_ref[...] = jnp.zeros_like(state_ref)

    x = x_ref[...]
    xn = (x * lax.rsqrt(jnp.mean(x * x, axis=-1, keepdims=True) + EPS) * nw_ref[...]).astype(BF16)

    def proj(lo, hi):
        return jnp.dot(xn, win_ref[:, lo:hi], preferred_element_type=F32)

    cw = 256

    def xbc_proj(k):
        xbc_ref[:, k * cw:(k + 1) * cw] = proj(O_XBC + k * cw, O_XBC + (k + 1) * cw)

    xbc_proj(0)
    for k in range(SSD_XBC // cw):
        if k + 1 < SSD_XBC // cw:
            xbc_proj(k + 1)
        cols = slice(k * cw, (k + 1) * cw)
        act = jax.nn.silu(_causal_conv(xbc_ref[:, cols], cw_ref, cols, cc_ref, SSD_CONV) + cb_ref[:, cols])
        if k < SSD_INNER // cw:
            xs_ref[:, cols] = act
        elif k < (SSD_INNER + SSD_GROUPS * SSD_STATE) // cw:
            o = k * cw - SSD_INNER
            b_ref[:, o:o + cw] = act.astype(BF16)
        else:
            o = k * cw - SSD_INNER - SSD_GROUPS * SSD_STATE
            c_ref[:, o:o + cw] = act.astype(BF16)

    for k in range(D_MODEL // cw):
        cols = slice(k * cw, (k + 1) * cw)
        p = proj(O_SC + k * cw, O_SC + (k + 1) * cw) * proj(O_SX + k * cw, O_SX + (k + 1) * cw)
        v = _causal_conv(p, scw_ref, cols, pc_ref, SCONV_K)
        sbv_ref[:, cols] = (proj(O_SB + k * cw, O_SB + (k + 1) * cw) * v).astype(BF16)

    dt = _softplus(proj(O_DT, O_DT + LANES) + dtb_ref[...])
    da = dt * (-jnp.exp(alog_ref[...]))
    hi = da.astype(BF16)
    r1 = da - hi.astype(F32)
    mid = r1.astype(BF16)
    lo = (r1 - mid.astype(F32)).astype(BF16)
    cs = jnp.dot(tril_ref[...], jnp.concatenate([hi, mid, lo], axis=1), preferred_element_type=F32)
    acum = (cs[:, 0:LANES] + cs[:, LANES:2 * LANES] + cs[:, 2 * LANES:3 * LANES]) * LOG2E
    lhs_a = _split2(acum)
    lhs_d = _split2(dt)
    for blk in range(N_BLOCKS):
        cols = slice(blk * BLOCK, (blk + 1) * BLOCK)
        af_ref[:, cols] = jnp.dot(lhs_a, e2_ref[:, cols], preferred_element_type=F32)
        xdt_ref[:, cols] = xs_ref[:, cols] * jnp.dot(lhs_d, e2_ref[:, cols], preferred_element_type=F32)

    rowb = lax.broadcasted_iota(jnp.int32, (CHUNK, BLOCK), 0)
    laneb = lax.broadcasted_iota(jnp.int32, (CHUNK, BLOCK), 1)
    pos = laneb % SSD_HEAD_DIM
    causal = rowb >= pos
    ident = rowb == pos
    lane1 = lax.broadcasted_iota(jnp.int32, (CHUNK, LANES), 1)
    half_masks = (lane1 < SSD_HEAD_DIM, lane1 >= SSD_HEAD_DIM)
    zero_half = jnp.zeros((CHUNK, LANES), BF16)

    n_chunks = ts // CHUNK
    pw = 256
    pieces = ([(zs_ref, O_Z, k, jax.nn.silu) for k in range(SSD_INNER // pw)]
              + [(ga_ref, O_GA, k, jax.nn.sigmoid) for k in range(D_MODEL // pw)]
              + [(gb_ref, O_GB, k, jax.nn.sigmoid) for k in range(D_MODEL // pw)])
    assert len(pieces) == n_chunks * SSD_GROUPS
    for c in range(n_chunks):
        rows = slice(c * CHUNK, (c + 1) * CHUNK)
        for g in range(SSD_GROUPS):
            ref, base, k, fn = pieces[c * SSD_GROUPS + g]
            ref[:, k * pw:(k + 1) * pw] = fn(proj(base + k * pw, base + (k + 1) * pw))
            gcols = slice(g * SSD_STATE, (g + 1) * SSD_STATE)
            c_g = c_ref[rows, gcols]
            b_g = b_ref[rows, gcols]
            cb2 = lax.dot_general(c_g, jnp.concatenate([b_g, b_g], axis=0),
                                  (((1,), (1,)), ((), ())), preferred_element_type=F32)
            cb4 = jnp.concatenate([cb2, cb2], axis=1)
            for half in range(2):
                blk = g * 2 + half
                cols = slice(blk * BLOCK, (blk + 1) * BLOCK)
                a_f = af_ref[rows, cols]
                a_s = jnp.sum(jnp.where(ident, a_f, 0.0), axis=0, keepdims=True)
                dec = jnp.exp2(jnp.where(causal, a_f - a_s, -jnp.inf))
                sc = (cb4 * dec).astype(BF16)
                xdt = xdt_ref[rows, cols]
                xdt_b = xdt.astype(BF16)
                blocks = []
                for i in range(HEADS_PER_BLOCK):
                    live = jnp.where(half_masks[i % 2], xdt_b[:, (i // 2) * LANES:(i // 2 + 1) * LANES], zero_half)
                    blocks.append(jnp.concatenate([live, zero_half] if i < 2 else [zero_half, live], axis=1))
                bd = jnp.concatenate(blocks, axis=0)
                y_diag = jnp.dot(sc, bd, preferred_element_type=F32)
                a_last = a_f[CHUNK - 1:CHUNK, :]
                st = state_ref[:, cols]
                y_off = jnp.dot(c_g, st.astype(BF16), preferred_element_type=F32) * jnp.exp2(a_f)
                y_ref[rows, cols] = y_diag + y_off
                wx = (xdt * jnp.exp2(a_last - a_f)).astype(BF16)
                upd = lax.dot_general(b_g, wx, (((0,), (0,)), ((), ())), preferred_element_type=F32)
                state_ref[:, cols] = st * jnp.exp2(a_last) + upd

    gw = SSD_INNER // SSD_GROUPS
    for g in range(SSD_GROUPS):
        cols = slice(g * gw, (g + 1) * gw)
        y = (y_ref[:, cols] + dskip_ref[:, cols] * xs_ref[:, cols]) * zs_ref[:, cols]
        r = lax.rsqrt(jnp.mean(y * y, axis=-1, keepdims=True) + EPS)
        yn_ref[:, cols] = (y * r * snw_ref[:, cols]).astype(BF16)
    br_a = jnp.dot(yn_ref[...], wssd_ref[...], preferred_element_type=F32)
    br_b = jnp.dot(sbv_ref[...], wsc_ref[...], preferred_element_type=F32)
    merged = (ga_ref[...] * br_a + gb_ref[...] * br_b).astype(BF16)
    out_ref[...] = x_ref[...] + jnp.dot(merged, wo_ref[...], preferred_element_type=F32)


def _ffn_kernel(h_ref, nw_ref, wg_ref, wu_ref, wd_ref, fw_ref, out_ref, *, piece):
    h = h_ref[...]
    hn = (h * lax.rsqrt(jnp.mean(h * h, axis=-1, keepdims=True) + EPS) * nw_ref[...]).astype(BF16)
    hidden = wg_ref.shape[1]
    acc = h
    for lo in range(0, hidden, piece):
        cols = slice(lo, min(lo + piece, hidden))
        g = jnp.dot(hn, wg_ref[:, cols], preferred_element_type=F32)
        u = jnp.dot(hn, wu_ref[:, cols], preferred_element_type=F32)
        act = (jax.nn.silu(g) * u).astype(BF16)
        acc = acc + jnp.dot(act, wd_ref[cols, :], preferred_element_type=F32)
    out_ref[...] = acc * lax.rsqrt(jnp.mean(acc * acc, axis=-1, keepdims=True) + EPS) * fw_ref[...]


def _const_spec(shape):
    return pl.BlockSpec(shape, lambda *_: (0,) * len(shape), pipeline_mode=pl.Buffered(1))


def _pad_lanes(v):
    return jnp.pad(v.astype(F32), (0, LANES - v.shape[0])).reshape(1, LANES)


def _transpose_kernel(wt_ref, out_ref):
    out_ref[...] = wt_ref[...].T.astype(BF16)


def _reorder_in_proj(w_in):
    d, n = w_in.shape
    cb = 512
    o_dt = SSD_INNER + SSD_XBC
    n_main = O_DT // cb
    n_head = o_dt // cb

    def src_row(i):
        return jnp.where(i < n_head, i * cb, jnp.where(i < n_main, i * cb + SSD_HEADS, o_dt))

    return pl.pallas_call(
        _transpose_kernel,
        grid=(n_main + 1,),
        in_specs=[pl.BlockSpec((pl.Element(cb), pl.Element(d)), lambda i: (pl.multiple_of(src_row(i), SSD_HEADS), 0))],
        out_specs=pl.BlockSpec((d, cb), lambda i: (0, i)),
        out_shape=jax.ShapeDtypeStruct((d, O_DT + LANES), BF16),
        compiler_params=pltpu.CompilerParams(dimension_semantics=("arbitrary",), vmem_limit_bytes=VMEM_LIMIT),
        name="reorder_in_proj",
    )(w_in.T)


def _mixer(x2, nw, w_in, cw, cb, dtb, alog, dskip, snw, w_ssd, scw, w_sc, w_o, batch, seq):
    ts = MIXER_TS
    nt = seq // ts
    w_in_r = _reorder_in_proj(w_in)
    head_of_lane = jnp.arange(SSD_INNER) // SSD_HEAD_DIM
    e1 = (jnp.arange(LANES)[:, None] == head_of_lane[None, :]).astype(BF16)
    e2 = jnp.concatenate([e1, e1], axis=0)
    r = jnp.arange(ts)
    tril = ((r[:, None] >= r[None, :]) & (r[:, None] // CHUNK == r[None, :] // CHUNK)).astype(BF16)
    dskip_f = jnp.repeat(dskip.astype(F32), SSD_HEAD_DIM).reshape(1, SSD_INNER)

    args = (x2, nw.reshape(1, D_MODEL), w_in_r, cw, cb.reshape(1, SSD_XBC), _pad_lanes(dtb),
            _pad_lanes(alog), dskip_f, snw.reshape(1, SSD_INNER), w_ssd.astype(BF16), scw,
            w_sc.astype(BF16), w_o.astype(BF16), e2, tril)
    in_specs = [pl.BlockSpec((ts, D_MODEL), lambda b, j: (b * nt + j, 0))]
    in_specs += [_const_spec(a.shape) for a in args[1:]]
    scratch = [
        pltpu.VMEM((ts, SSD_XBC), F32),
        pltpu.VMEM((SUBLANES, SSD_XBC), F32),
        pltpu.VMEM((SUBLANES, D_MODEL), F32),
        pltpu.VMEM((ts, SSD_INNER), F32),
        pltpu.VMEM((ts, SSD_INNER), F32),
        pltpu.VMEM((ts, SSD_GROUPS * SSD_STATE), BF16),
        pltpu.VMEM((ts, SSD_GROUPS * SSD_STATE), BF16),
        pltpu.VMEM((ts, SSD_INNER), F32),
        pltpu.VMEM((ts, SSD_INNER), F32),
        pltpu.VMEM((ts, SSD_INNER), F32),
        pltpu.VMEM((ts, SSD_INNER), BF16),
        pltpu.VMEM((ts, D_MODEL), BF16),
        pltpu.VMEM((ts, D_MODEL), F32),
        pltpu.VMEM((ts, D_MODEL), F32),
        pltpu.VMEM((SSD_STATE, SSD_INNER), F32),
    ]
    return pl.pallas_call(
        functools.partial(_mixer_kernel, ts=ts),
        grid=(batch, nt),
        in_specs=in_specs,
        out_specs=pl.BlockSpec((ts, D_MODEL), lambda b, j: (b * nt + j, 0)),
        out_shape=jax.ShapeDtypeStruct(x2.shape, F32),
        scratch_shapes=scratch,
        compiler_params=pltpu.CompilerParams(
            dimension_semantics=("arbitrary", "arbitrary"), vmem_limit_bytes=VMEM_LIMIT),
        name="mixer",
    )(*args)


def _ffn(h, nw, w_gate, w_up, w_down, fw):
    tm = FFN_TM
    t = h.shape[0]
    args = (h, nw.reshape(1, D_MODEL), w_gate.astype(BF16), w_up.astype(BF16), w_down.astype(BF16),
            fw.reshape(1, D_MODEL))
    in_specs = [pl.BlockSpec((tm, D_MODEL), lambda i: (i, 0))]
    in_specs += [_const_spec(a.shape) for a in args[1:]]
    return pl.pallas_call(
        functools.partial(_ffn_kernel, piece=FFN_PIECE),
        grid=(t // tm,),
        in_specs=in_specs,
        out_specs=pl.BlockSpec((tm, D_MODEL), lambda i: (i, 0)),
        out_shape=jax.ShapeDtypeStruct(h.shape, F32),
        compiler_params=pltpu.CompilerParams(
            dimension_semantics=("arbitrary",), vmem_limit_bytes=VMEM_LIMIT),
        name="ffn",
    )(*args)


def kernel(x, norm_mix_w, w_in, ssd_conv_w, ssd_conv_b, dt_bias, a_log, d_skip, ssd_norm_w, w_ssd_proj,
           sconv_w, w_sconv_proj, w_o, norm_ffn_w, w_gate, w_up, w_down, final_norm_w):
    batch, seq, d = x.shape
    assert w_in.shape[0] == 1
    h = x.reshape(batch * seq, d)
    h = _mixer(h, norm_mix_w[0], w_in[0], ssd_conv_w[0], ssd_conv_b[0], dt_bias[0], a_log[0],
               d_skip[0], ssd_norm_w[0], w_ssd_proj[0], sconv_w[0], w_sconv_proj[0], w_o[0],
               batch, seq)
    h = _ffn(h, norm_ffn_w[0], w_gate[0], w_up[0], w_down[0], final_norm_w)
    return h.reshape(batch, seq, d)
```

```python
import functools

import jax
import jax.numpy as jnp
from jax import lax
from jax.experimental import pallas as pl
from jax.experimental.pallas import tpu as pltpu

F32 = jnp.float32
BF16 = jnp.bfloat16

EPS = 1e-6
LOG2E = 1.4426950408889634
D_MODEL = 1024
SSD_INNER = 2048
SSD_HEAD_DIM = 64
SSD_HEADS = 32
SSD_GROUPS = 4
SSD_STATE = 128
SSD_CONV = 4
SSD_XBC = SSD_INNER + 2 * SSD_GROUPS * SSD_STATE
SCONV_K = 3
CHUNK = 64
LANES = 128
SUBLANES = 8
HEADS_PER_BLOCK = 4
BLOCK = HEADS_PER_BLOCK * SSD_HEAD_DIM
N_BLOCKS = SSD_INNER // BLOCK

O_Z = 0
O_XBC = O_Z + SSD_INNER
O_SB = O_XBC + SSD_XBC
O_SC = O_SB + D_MODEL
O_SX = O_SC + D_MODEL
O_GA = O_SX + D_MODEL
O_GB = O_GA + D_MODEL
O_DT = O_GB + D_MODEL

MIXER_TS = 256
FFN_TM = 1024
MXU_TILE = 256
FFN_PIECE = 4 * MXU_TILE
VMEM_LIMIT = 58 * 1024 * 1024


def _softplus(v):
    return jnp.maximum(v, 0.0) + jnp.log(1.0 + jnp.exp(-jnp.abs(v)))


def _split2(v):
    hi = v.astype(BF16)
    lo = (v - hi.astype(F32)).astype(BF16)
    return jnp.concatenate([hi, lo], axis=1)


SLAB = 32
PHASES = SLAB // SUBLANES


def _time_of_row(r):
    return (r // SLAB) * SLAB + PHASES * (r % SUBLANES) + (r % SLAB) // SUBLANES


def _to_interleaved(ref, stage_ref):
    n, rows, _ = stage_ref.shape
    for j in range(n):
        stage_ref[j] = ref[:, j * LANES:(j + 1) * LANES]
    return jnp.concatenate(
        [jnp.concatenate([stage_ref[j, pl.ds(s0 + k, SUBLANES, stride=PHASES), :]
                          for s0 in range(0, rows, SLAB) for k in range(PHASES)], axis=0)
         for j in range(n)], axis=1)


def _from_interleaved(val, stage_ref):
    n, rows, _ = stage_ref.shape
    for j in range(n):
        for s0 in range(0, rows, SLAB):
            for k in range(PHASES):
                stage_ref[j, pl.ds(s0 + k, SUBLANES, stride=PHASES), :] = (
                    val[s0 + SUBLANES * k:s0 + SUBLANES * (k + 1), j * LANES:(j + 1) * LANES])


def _shift_rows(s, carry_row):
    ts = s.shape[0]
    last = SLAB - SUBLANES
    tails = jnp.concatenate([s[s0 + last:s0 + SLAB] for s0 in range(0, ts, SLAB)], axis=0)
    rolled = pltpu.roll(tails, 1, axis=0)
    head = rolled[0:SUBLANES]
    row = lax.broadcasted_iota(jnp.int32, head.shape, 0)
    rolled = jnp.concatenate([jnp.where(row == 0, carry_row, head), rolled[SUBLANES:]], axis=0)
    out = []
    for i, s0 in enumerate(range(0, ts, SLAB)):
        out += [rolled[i * SUBLANES:(i + 1) * SUBLANES], s[s0:s0 + last]]
    return jnp.concatenate(out, axis=0)


def _causal_conv(x, w_ref, cols, carry_ref, taps):
    ts = x.shape[0]
    s = w_ref[0:1, cols] * x
    for i in range(1, taps):
        carry = carry_ref[i - 1:i, cols]
        carry_ref[i - 1:i, cols] = s[ts - 1:ts]
        s = w_ref[i:i + 1, cols] * x + _shift_rows(s, carry)
    return s


def _mixer_kernel(x_ref, nw_ref, win_ref, cw_ref, cb_ref, dtb_ref, alog_ref, dskip_ref, snw_ref,
                  wssd_ref, scw_ref, wsc_ref, wo_ref, e2_ref, tril_ref, out_ref,
                  stage_ref, xbc_ref, cc_ref, pc_ref, zs_ref, xs_ref, b_ref, c_ref, af_ref, xdt_ref, y_ref,
                  yn_ref, sbv_ref, ga_ref, gb_ref, state_ref, *, ts):
    j = pl.program_id(1)

    @pl.when(j == 0)
    def _():
        cc_ref[...] = jnp.zeros_like(cc_ref)
        pc_ref[...] = jnp.zeros_like(pc_ref)
        state_ref[...] = jnp.zeros_like(state_ref)

    x = _to_interleaved(x_ref, stage_ref)
    xn = (x * lax.rsqrt(jnp.mean(x * x, axis=-1, keepdims=True) + EPS) * nw_ref[...]).astype(BF16)

    def proj(lo, hi):
        return jnp.dot(xn, win_ref[:, lo:hi], preferred_element_type=F32)

    cw = 256

    def xbc_proj(k):
        xbc_ref[:, k * cw:(k + 1) * cw] = proj(O_XBC + k * cw, O_XBC + (k + 1) * cw)

    xbc_proj(0)
    for k in range(SSD_XBC // cw):
        if k + 1 < SSD_XBC // cw:
            xbc_proj(k + 1)
        cols = slice(k * cw, (k + 1) * cw)
        act = jax.nn.silu(_causal_conv(xbc_ref[:, cols], cw_ref, cols, cc_ref, SSD_CONV) + cb_ref[:, cols])
        if k < SSD_INNER // cw:
            xs_ref[:, cols] = act
        elif k < (SSD_INNER + SSD_GROUPS * SSD_STATE) // cw:
            o = k * cw - SSD_INNER
            b_ref[:, o:o + cw] = act.astype(BF16)
        else:
            o = k * cw - SSD_INNER - SSD_GROUPS * SSD_STATE
            c_ref[:, o:o + cw] = act.astype(BF16)

    for k in range(D_MODEL // cw):
        cols = slice(k * cw, (k + 1) * cw)
        p = proj(O_SC + k * cw, O_SC + (k + 1) * cw) * proj(O_SX + k * cw, O_SX + (k + 1) * cw)
        v = _causal_conv(p, scw_ref, cols, pc_ref, SCONV_K)
        sbv_ref[:, cols] = (proj(O_SB + k * cw, O_SB + (k + 1) * cw) * v).astype(BF16)

    dt = _softplus(proj(O_DT, O_DT + LANES) + dtb_ref[...])
    da = dt * (-jnp.exp(alog_ref[...]))
    hi = da.astype(BF16)
    r1 = da - hi.astype(F32)
    mid = r1.astype(BF16)
    lo = (r1 - mid.astype(F32)).astype(BF16)
    cs = jnp.dot(tril_ref[...], jnp.concatenate([hi, mid, lo], axis=1), preferred_element_type=F32)
    acum = (cs[:, 0:LANES] + cs[:, LANES:2 * LANES] + cs[:, 2 * LANES:3 * LANES]) * LOG2E
    lhs_a = _split2(acum)
    lhs_d = _split2(dt)
    for blk in range(N_BLOCKS):
        cols = slice(blk * BLOCK, (blk + 1) * BLOCK)
        af_ref[:, cols] = jnp.dot(lhs_a, e2_ref[:, cols], preferred_element_type=F32)
        xdt_ref[:, cols] = xs_ref[:, cols] * jnp.dot(lhs_d, e2_ref[:, cols], preferred_element_type=F32)

    rowb = lax.broadcasted_iota(jnp.int32, (CHUNK, BLOCK), 0)
    laneb = lax.broadcasted_iota(jnp.int32, (CHUNK, BLOCK), 1)
    pos = laneb % SSD_HEAD_DIM
    causal = _time_of_row(rowb) >= _time_of_row(pos)
    ident = rowb == pos
    lane1 = lax.broadcasted_iota(jnp.int32, (CHUNK, LANES), 1)
    half_masks = (lane1 < SSD_HEAD_DIM, lane1 >= SSD_HEAD_DIM)
    zero_half = jnp.zeros((CHUNK, LANES), BF16)

    n_chunks = ts // CHUNK
    pw = 256
    pieces = ([(zs_ref, O_Z, k, jax.nn.silu) for k in range(SSD_INNER // pw)]
              + [(ga_ref, O_GA, k, jax.nn.sigmoid) for k in range(D_MODEL // pw)]
              + [(gb_ref, O_GB, k, jax.nn.sigmoid) for k in range(D_MODEL // pw)])
    assert len(pieces) == n_chunks * SSD_GROUPS
    for c in range(n_chunks):
        rows = slice(c * CHUNK, (c + 1) * CHUNK)
        for g in range(SSD_GROUPS):
            ref, base, k, fn = pieces[c * SSD_GROUPS + g]
            ref[:, k * pw:(k + 1) * pw] = fn(proj(base + k * pw, base + (k + 1) * pw))
            gcols = slice(g * SSD_STATE, (g + 1) * SSD_STATE)
            c_g = c_ref[rows, gcols]
            b_g = b_ref[rows, gcols]
            cb2 = lax.dot_general(c_g, jnp.concatenate([b_g, b_g], axis=0),
                                  (((1,), (1,)), ((), ())), preferred_element_type=F32)
            cb4 = jnp.concatenate([cb2, cb2], axis=1)
            for half in range(2):
                blk = g * 2 + half
                cols = slice(blk * BLOCK, (blk + 1) * BLOCK)
                a_f = af_ref[rows, cols]
                a_s = jnp.sum(jnp.where(ident, a_f, 0.0), axis=0, keepdims=True)
                dec = jnp.exp2(jnp.where(causal, a_f - a_s, -jnp.inf))
                sc = (cb4 * dec).astype(BF16)
                xdt = xdt_ref[rows, cols]
                xdt_b = xdt.astype(BF16)
                blocks = []
                for i in range(HEADS_PER_BLOCK):
                    live = jnp.where(half_masks[i % 2], xdt_b[:, (i // 2) * LANES:(i // 2 + 1) * LANES], zero_half)
                    blocks.append(jnp.concatenate([live, zero_half] if i < 2 else [zero_half, live], axis=1))
                bd = jnp.concatenate(blocks, axis=0)
                y_diag = jnp.dot(sc, bd, preferred_element_type=F32)
                a_last = a_f[CHUNK - 1:CHUNK, :]
                st = state_ref[:, cols]
                y_off = jnp.dot(c_g, st.astype(BF16), preferred_element_type=F32) * jnp.exp2(a_f)
                y_ref[rows, cols] = y_diag + y_off
                wx = (xdt * jnp.exp2(a_last - a_f)).astype(BF16)
                upd = lax.dot_general(b_g, wx, (((0,), (0,)), ((), ())), preferred_element_type=F32)
                state_ref[:, cols] = st * jnp.exp2(a_last) + upd

    gw = SSD_INNER // SSD_GROUPS
    for g in range(SSD_GROUPS):
        cols = slice(g * gw, (g + 1) * gw)
        y = (y_ref[:, cols] + dskip_ref[:, cols] * xs_ref[:, cols]) * zs_ref[:, cols]
        r = lax.rsqrt(jnp.mean(y * y, axis=-1, keepdims=True) + EPS)
        yn_ref[:, cols] = (y * r * snw_ref[:, cols]).astype(BF16)
    br_a = jnp.dot(yn_ref[...], wssd_ref[...], preferred_element_type=F32)
    br_b = jnp.dot(sbv_ref[...], wsc_ref[...], preferred_element_type=F32)
    merged = (ga_ref[...] * br_a + gb_ref[...] * br_b).astype(BF16)
    _from_interleaved(jnp.dot(merged, wo_ref[...], preferred_element_type=F32), stage_ref)
    for j in range(D_MODEL // LANES):
        cols = slice(j * LANES, (j + 1) * LANES)
        out_ref[:, cols] = x_ref[:, cols] + stage_ref[j]


def _ffn_kernel(h_ref, nw_ref, wg_ref, wu_ref, wd_ref, fw_ref, out_ref, *, piece):
    h = h_ref[...]
    hn = (h * lax.rsqrt(jnp.mean(h * h, axis=-1, keepdims=True) + EPS) * nw_ref[...]).astype(BF16)
    hidden = wg_ref.shape[1]
    acc = h
    for lo in range(0, hidden, piece):
        cols = slice(lo, min(lo + piece, hidden))
        g = jnp.dot(hn, wg_ref[:, cols], preferred_element_type=F32)
        u = jnp.dot(hn, wu_ref[:, cols], preferred_element_type=F32)
        act = (jax.nn.silu(g) * u).astype(BF16)
        acc = acc + jnp.dot(act, wd_ref[cols, :], preferred_element_type=F32)
    out_ref[...] = acc * lax.rsqrt(jnp.mean(acc * acc, axis=-1, keepdims=True) + EPS) * fw_ref[...]


def _const_spec(shape):
    return pl.BlockSpec(shape, lambda *_: (0,) * len(shape), pipeline_mode=pl.Buffered(1))


def _pad_lanes(v):
    return jnp.pad(v.astype(F32), (0, LANES - v.shape[0])).reshape(1, LANES)


def _transpose_kernel(wt_ref, out_ref):
    out_ref[...] = wt_ref[...].T.astype(BF16)


def _reorder_in_proj(w_in):
    d, n = w_in.shape
    cb = 512
    o_dt = SSD_INNER + SSD_XBC
    n_main = O_DT // cb
    n_head = o_dt // cb

    def src_row(i):
        return jnp.where(i < n_head, i * cb, jnp.where(i < n_main, i * cb + SSD_HEADS, o_dt))

    return pl.pallas_call(
        _transpose_kernel,
        grid=(n_main + 1,),
        in_specs=[pl.BlockSpec((pl.Element(cb), pl.Element(d)), lambda i: (pl.multiple_of(src_row(i), SSD_HEADS), 0))],
        out_specs=pl.BlockSpec((d, cb), lambda i: (0, i)),
        out_shape=jax.ShapeDtypeStruct((d, O_DT + LANES), BF16),
        compiler_params=pltpu.CompilerParams(dimension_semantics=("arbitrary",), vmem_limit_bytes=VMEM_LIMIT),
        name="reorder_in_proj",
    )(w_in.T)


def _mixer(x2, nw, w_in, cw, cb, dtb, alog, dskip, snw, w_ssd, scw, w_sc, w_o, batch, seq):
    ts = MIXER_TS
    nt = seq // ts
    w_in_r = _reorder_in_proj(w_in)
    head_of_lane = jnp.arange(SSD_INNER) // SSD_HEAD_DIM
    e1 = (jnp.arange(LANES)[:, None] == head_of_lane[None, :]).astype(BF16)
    e2 = jnp.concatenate([e1, e1], axis=0)
    r = jnp.arange(ts)
    t = _time_of_row(r)
    tril = ((t[:, None] >= t[None, :]) & (r[:, None] // CHUNK == r[None, :] // CHUNK)).astype(BF16)
    dskip_f = jnp.repeat(dskip.astype(F32), SSD_HEAD_DIM).reshape(1, SSD_INNER)

    args = (x2, nw.reshape(1, D_MODEL), w_in_r, cw, cb.reshape(1, SSD_XBC), _pad_lanes(dtb),
            _pad_lanes(alog), dskip_f, snw.reshape(1, SSD_INNER), w_ssd.astype(BF16), scw,
            w_sc.astype(BF16), w_o.astype(BF16), e2, tril)
    in_specs = [pl.BlockSpec((ts, D_MODEL), lambda b, j: (b * nt + j, 0))]
    in_specs += [_const_spec(a.shape) for a in args[1:]]
    scratch = [
        pltpu.VMEM((D_MODEL // LANES, ts, LANES), F32),
        pltpu.VMEM((ts, SSD_XBC), F32),
        pltpu.VMEM((SUBLANES, SSD_XBC), F32),
        pltpu.VMEM((SUBLANES, D_MODEL), F32),
        pltpu.VMEM((ts, SSD_INNER), F32),
        pltpu.VMEM((ts, SSD_INNER), F32),
        pltpu.VMEM((ts, SSD_GROUPS * SSD_STATE), BF16),
        pltpu.VMEM((ts, SSD_GROUPS * SSD_STATE), BF16),
        pltpu.VMEM((ts, SSD_INNER), F32),
        pltpu.VMEM((ts, SSD_INNER), F32),
        pltpu.VMEM((ts, SSD_INNER), F32),
        pltpu.VMEM((ts, SSD_INNER), BF16),
        pltpu.VMEM((ts, D_MODEL), BF16),
        pltpu.VMEM((ts, D_MODEL), F32),
        pltpu.VMEM((ts, D_MODEL), F32),
        pltpu.VMEM((SSD_STATE, SSD_INNER), F32),
    ]
    return pl.pallas_call(
        functools.partial(_mixer_kernel, ts=ts),
        grid=(batch, nt),
        in_specs=in_specs,
        out_specs=pl.BlockSpec((ts, D_MODEL), lambda b, j: (b * nt + j, 0)),
        out_shape=jax.ShapeDtypeStruct(x2.shape, F32),
        scratch_shapes=scratch,
        compiler_params=pltpu.CompilerParams(
            dimension_semantics=("arbitrary", "arbitrary"), vmem_limit_bytes=VMEM_LIMIT),
        name="mixer",
    )(*args)


def _ffn(h, nw, w_gate, w_up, w_down, fw):
    tm = FFN_TM
    t = h.shape[0]
    args = (h, nw.reshape(1, D_MODEL), w_gate.astype(BF16), w_up.astype(BF16), w_down.astype(BF16),
            fw.reshape(1, D_MODEL))
    in_specs = [pl.BlockSpec((tm, D_MODEL), lambda i: (i, 0))]
    in_specs += [_const_spec(a.shape) for a in args[1:]]
    return pl.pallas_call(
        functools.partial(_ffn_kernel, piece=FFN_PIECE),
        grid=(t // tm,),
        in_specs=in_specs,
        out_specs=pl.BlockSpec((tm, D_MODEL), lambda i: (i, 0)),
        out_shape=jax.ShapeDtypeStruct(h.shape, F32),
        compiler_params=pltpu.CompilerParams(
            dimension_semantics=("arbitrary",), vmem_limit_bytes=VMEM_LIMIT),
        name="ffn",
    )(*args)


def kernel(x, norm_mix_w, w_in, ssd_conv_w, ssd_conv_b, dt_bias, a_log, d_skip, ssd_norm_w, w_ssd_proj,
           sconv_w, w_sconv_proj, w_o, norm_ffn_w, w_gate, w_up, w_down, final_norm_w):
    batch, seq, d = x.shape
    assert w_in.shape[0] == 1
    h = x.reshape(batch * seq, d)
    h = _mixer(h, norm_mix_w[0], w_in[0], ssd_conv_w[0], ssd_conv_b[0], dt_bias[0], a_log[0],
               d_skip[0], ssd_norm_w[0], w_ssd_proj[0], sconv_w[0], w_sconv_proj[0], w_o[0],
               batch, seq)
    h = _ffn(h, norm_ffn_w[0], w_gate[0], w_up[0], w_down[0], final_norm_w)
    return h.reshape(batch, seq, d)
```

```python
import functools

import jax
import jax.numpy as jnp
from jax import lax
from jax.experimental import pallas as pl
from jax.experimental.pallas import tpu as pltpu

F32 = jnp.float32
BF16 = jnp.bfloat16

EPS = 1e-6
LOG2E = 1.4426950408889634
D_MODEL = 1024
SSD_INNER = 2048
SSD_HEAD_DIM = 64
SSD_HEADS = 32
SSD_GROUPS = 4
SSD_STATE = 128
SSD_CONV = 4
SSD_XBC = SSD_INNER + 2 * SSD_GROUPS * SSD_STATE
SCONV_K = 3
CHUNK = 64
LANES = 128
SUBLANES = 8
HEADS_PER_BLOCK = 4
BLOCK = HEADS_PER_BLOCK * SSD_HEAD_DIM
N_BLOCKS = SSD_INNER // BLOCK

O_Z = 0
O_XBC = O_Z + SSD_INNER
O_SB = O_XBC + SSD_XBC
O_SC = O_SB + D_MODEL
O_SX = O_SC + D_MODEL
O_GA = O_SX + D_MODEL
O_GB = O_GA + D_MODEL
O_DT = O_GB + D_MODEL

MIXER_TS = 256
FFN_TM = 1024
MXU_TILE = 256
FFN_PIECE = 4 * MXU_TILE
FFN_PARTS = 4
VMEM_LIMIT = 58 * 1024 * 1024


def _softplus(v):
    return jnp.maximum(v, 0.0) + jnp.log(1.0 + jnp.exp(-jnp.abs(v)))


def _split2(v):
    hi = v.astype(BF16)
    lo = (v - hi.astype(F32)).astype(BF16)
    return jnp.concatenate([hi, lo], axis=1)


SLAB = 32
PHASES = SLAB // SUBLANES


def _time_of_row(r):
    return (r // SLAB) * SLAB + PHASES * (r % SUBLANES) + (r % SLAB) // SUBLANES


def _to_interleaved(ref, stage_ref):
    n, rows, _ = stage_ref.shape
    for j in range(n):
        stage_ref[j] = ref[:, j * LANES:(j + 1) * LANES]
    return jnp.concatenate(
        [jnp.concatenate([stage_ref[j, pl.ds(s0 + k, SUBLANES, stride=PHASES), :]
                          for s0 in range(0, rows, SLAB) for k in range(PHASES)], axis=0)
         for j in range(n)], axis=1)


def _from_interleaved(val, stage_ref):
    n, rows, _ = stage_ref.shape
    for j in range(n):
        for s0 in range(0, rows, SLAB):
            for k in range(PHASES):
                stage_ref[j, pl.ds(s0 + k, SUBLANES, stride=PHASES), :] = (
                    val[s0 + SUBLANES * k:s0 + SUBLANES * (k + 1), j * LANES:(j + 1) * LANES])


def _shift_rows(s, carry_row):
    ts = s.shape[0]
    last = SLAB - SUBLANES
    tails = jnp.concatenate([s[s0 + last:s0 + SLAB] for s0 in range(0, ts, SLAB)], axis=0)
    rolled = pltpu.roll(tails, 1, axis=0)
    head = rolled[0:SUBLANES]
    row = lax.broadcasted_iota(jnp.int32, head.shape, 0)
    rolled = jnp.concatenate([jnp.where(row == 0, carry_row, head), rolled[SUBLANES:]], axis=0)
    out = []
    for i, s0 in enumerate(range(0, ts, SLAB)):
        out += [rolled[i * SUBLANES:(i + 1) * SUBLANES], s[s0:s0 + last]]
    return jnp.concatenate(out, axis=0)


def _causal_conv(x, w_ref, cols, carry_ref, taps):
    ts = x.shape[0]
    s = w_ref[0:1, cols] * x
    for i in range(1, taps):
        carry = carry_ref[i - 1:i, cols]
        carry_ref[i - 1:i, cols] = s[ts - 1:ts]
        s = w_ref[i:i + 1, cols] * x + _shift_rows(s, carry)
    return s


def _mixer_kernel(x_ref, nw_ref, win_ref, cw_ref, cb_ref, dtb_ref, alog_ref, dskip_ref, snw_ref,
                  wssd_ref, scw_ref, wsc_ref, wo_ref, e2_ref, tril_ref, out_ref,
                  stage_ref, xbc_ref, cc_ref, pc_ref, zs_ref, xs_ref, b_ref, c_ref, af_ref, xdt_ref, y_ref,
                  yn_ref, sbv_ref, ga_ref, gb_ref, state_ref, *, ts):
    j = pl.program_id(1)

    @pl.when(j == 0)
    def _():
        cc_ref[...] = jnp.zeros_like(cc_ref)
        pc_ref[...] = jnp.zeros_like(pc_ref)
        state_ref[...] = jnp.zeros_like(state_ref)

    x = _to_interleaved(x_ref, stage_ref)
    xn = (x * lax.rsqrt(jnp.mean(x * x, axis=-1, keepdims=True) + EPS) * nw_ref[...]).astype(BF16)

    def proj(lo, hi):
        return jnp.dot(xn, win_ref[:, lo:hi], preferred_element_type=F32)

    cw = 256

    def xbc_proj(k):
        xbc_ref[:, k * cw:(k + 1) * cw] = proj(O_XBC + k * cw, O_XBC + (k + 1) * cw)

    xbc_proj(0)
    for k in range(SSD_XBC // cw):
        if k + 1 < SSD_XBC // cw:
            xbc_proj(k + 1)
        cols = slice(k * cw, (k + 1) * cw)
        act = jax.nn.silu(_causal_conv(xbc_ref[:, cols], cw_ref, cols, cc_ref, SSD_CONV) + cb_ref[:, cols])
        if k < SSD_INNER // cw:
            xs_ref[:, cols] = act
        elif k < (SSD_INNER + SSD_GROUPS * SSD_STATE) // cw:
            o = k * cw - SSD_INNER
            b_ref[:, o:o + cw] = act.astype(BF16)
        else:
            o = k * cw - SSD_INNER - SSD_GROUPS * SSD_STATE
            c_ref[:, o:o + cw] = act.astype(BF16)

    for k in range(D_MODEL // cw):
        cols = slice(k * cw, (k + 1) * cw)
        p = proj(O_SC + k * cw, O_SC + (k + 1) * cw) * proj(O_SX + k * cw, O_SX + (k + 1) * cw)
        v = _causal_conv(p, scw_ref, cols, pc_ref, SCONV_K)
        sbv_ref[:, cols] = (proj(O_SB + k * cw, O_SB + (k + 1) * cw) * v).astype(BF16)

    dt = _softplus(proj(O_DT, O_DT + LANES) + dtb_ref[...])
    da = dt * (-jnp.exp(alog_ref[...]))
    hi = da.astype(BF16)
    r1 = da - hi.astype(F32)
    mid = r1.astype(BF16)
    lo = (r1 - mid.astype(F32)).astype(BF16)
    cs = jnp.dot(tril_ref[...], jnp.concatenate([hi, mid, lo], axis=1), preferred_element_type=F32)
    acum = (cs[:, 0:LANES] + cs[:, LANES:2 * LANES] + cs[:, 2 * LANES:3 * LANES]) * LOG2E
    lhs_a = _split2(acum)
    lhs_d = _split2(dt)
    for blk in range(N_BLOCKS):
        cols = slice(blk * BLOCK, (blk + 1) * BLOCK)
        af_ref[:, cols] = jnp.dot(lhs_a, e2_ref[:, cols], preferred_element_type=F32)
        xdt_ref[:, cols] = xs_ref[:, cols] * jnp.dot(lhs_d, e2_ref[:, cols], preferred_element_type=F32)

    rowb = lax.broadcasted_iota(jnp.int32, (CHUNK, BLOCK), 0)
    laneb = lax.broadcasted_iota(jnp.int32, (CHUNK, BLOCK), 1)
    pos = laneb % SSD_HEAD_DIM
    causal = _time_of_row(rowb) >= _time_of_row(pos)
    ident = rowb == pos
    lane1 = lax.broadcasted_iota(jnp.int32, (CHUNK, LANES), 1)
    half_masks = (lane1 < SSD_HEAD_DIM, lane1 >= SSD_HEAD_DIM)
    zero_half = jnp.zeros((CHUNK, LANES), BF16)

    n_chunks = ts // CHUNK
    pw = 256
    pieces = ([(zs_ref, O_Z, k, jax.nn.silu) for k in range(SSD_INNER // pw)]
              + [(ga_ref, O_GA, k, jax.nn.sigmoid) for k in range(D_MODEL // pw)]
              + [(gb_ref, O_GB, k, jax.nn.sigmoid) for k in range(D_MODEL // pw)])
    assert len(pieces) == n_chunks * SSD_GROUPS
    for c in range(n_chunks):
        rows = slice(c * CHUNK, (c + 1) * CHUNK)
        for g in range(SSD_GROUPS):
            ref, base, k, fn = pieces[c * SSD_GROUPS + g]
            ref[:, k * pw:(k + 1) * pw] = fn(proj(base + k * pw, base + (k + 1) * pw))
            gcols = slice(g * SSD_STATE, (g + 1) * SSD_STATE)
            c_g = c_ref[rows, gcols]
            b_g = b_ref[rows, gcols]
            cb2 = lax.dot_general(c_g, jnp.concatenate([b_g, b_g], axis=0),
                                  (((1,), (1,)), ((), ())), preferred_element_type=F32)
            cb4 = jnp.concatenate([cb2, cb2], axis=1)
            for half in range(2):
                blk = g * 2 + half
                cols = slice(blk * BLOCK, (blk + 1) * BLOCK)
                a_f = af_ref[rows, cols]
                a_s = jnp.sum(jnp.where(ident, a_f, 0.0), axis=0, keepdims=True)
                dec = jnp.exp2(jnp.where(causal, a_f - a_s, -jnp.inf))
                sc = (cb4 * dec).astype(BF16)
                xdt = xdt_ref[rows, cols]
                xdt_b = xdt.astype(BF16)
                blocks = []
                for i in range(HEADS_PER_BLOCK):
                    live = jnp.where(half_masks[i % 2], xdt_b[:, (i // 2) * LANES:(i // 2 + 1) * LANES], zero_half)
                    blocks.append(jnp.concatenate([live, zero_half] if i < 2 else [zero_half, live], axis=1))
                bd = jnp.concatenate(blocks, axis=0)
                y_diag = jnp.dot(sc, bd, preferred_element_type=F32)
                a_last = a_f[CHUNK - 1:CHUNK, :]
                st = state_ref[:, cols]
                y_off = jnp.dot(c_g, st.astype(BF16), preferred_element_type=F32) * jnp.exp2(a_f)
                y_ref[rows, cols] = y_diag + y_off
                wx = (xdt * jnp.exp2(a_last - a_f)).astype(BF16)
                upd = lax.dot_general(b_g, wx, (((0,), (0,)), ((), ())), preferred_element_type=F32)
                state_ref[:, cols] = st * jnp.exp2(a_last) + upd

    gw = SSD_INNER // SSD_GROUPS
    for g in range(SSD_GROUPS):
        cols = slice(g * gw, (g + 1) * gw)
        y = (y_ref[:, cols] + dskip_ref[:, cols] * xs_ref[:, cols]) * zs_ref[:, cols]
        r = lax.rsqrt(jnp.mean(y * y, axis=-1, keepdims=True) + EPS)
        yn_ref[:, cols] = (y * r * snw_ref[:, cols]).astype(BF16)
    br_a = jnp.dot(yn_ref[...], wssd_ref[...], preferred_element_type=F32)
    br_b = jnp.dot(sbv_ref[...], wsc_ref[...], preferred_element_type=F32)
    merged = (ga_ref[...] * br_a + gb_ref[...] * br_b).astype(BF16)
    _from_interleaved(jnp.dot(merged, wo_ref[...], preferred_element_type=F32), stage_ref)
    for j in range(D_MODEL // LANES):
        cols = slice(j * LANES, (j + 1) * LANES)
        out_ref[:, cols] = x_ref[:, cols] + stage_ref[j]


def _ffn_kernel(h_ref, nw_ref, wg_ref, wu_ref, wd_ref, fw_ref, out_ref, *, piece, parts):
    hidden = wg_ref.shape[1]
    rows = h_ref.shape[0]
    groups = [slice(i * rows // parts, (i + 1) * rows // parts) for i in range(parts)]
    hs = [h_ref[r, :] for r in groups]
    hns = [(h * lax.rsqrt(jnp.mean(h * h, axis=-1, keepdims=True) + EPS) * nw_ref[...]).astype(BF16) for h in hs]
    accs = list(hs)
    for lo in range(0, hidden, piece):
        cols = slice(lo, min(lo + piece, hidden))
        for i, hn in enumerate(hns):
            g = jnp.dot(hn, wg_ref[:, cols], preferred_element_type=F32)
            u = jnp.dot(hn, wu_ref[:, cols], preferred_element_type=F32)
            act = (jax.nn.silu(g) * u).astype(BF16)
            accs[i] = accs[i] + jnp.dot(act, wd_ref[cols, :], preferred_element_type=F32)
    for r, acc in zip(groups, accs):
        out_ref[r, :] = acc * lax.rsqrt(jnp.mean(acc * acc, axis=-1, keepdims=True) + EPS) * fw_ref[...]


def _const_spec(shape):
    return pl.BlockSpec(shape, lambda *_: (0,) * len(shape), pipeline_mode=pl.Buffered(1))


def _pad_lanes(v):
    return jnp.pad(v.astype(F32), (0, LANES - v.shape[0])).reshape(1, LANES)


def _transpose_kernel(wt_ref, out_ref):
    out_ref[...] = wt_ref[...].T.astype(BF16)


def _reorder_in_proj(w_in):
    d, n = w_in.shape
    cb = 1024
    o_dt = SSD_INNER + SSD_XBC
    n_main = O_DT // cb
    n_head = o_dt // cb

    def src_row(i):
        return jnp.where(i < n_head, i * cb, jnp.where(i < n_main, i * cb + SSD_HEADS, o_dt))

    return pl.pallas_call(
        _transpose_kernel,
        grid=(n_main + 1,),
        in_specs=[pl.BlockSpec((pl.Element(cb), pl.Element(d)), lambda i: (pl.multiple_of(src_row(i), SSD_HEADS), 0))],
        out_specs=pl.BlockSpec((d, cb), lambda i: (0, i)),
        out_shape=jax.ShapeDtypeStruct((d, O_DT + LANES), BF16),
        compiler_params=pltpu.CompilerParams(dimension_semantics=("arbitrary",), vmem_limit_bytes=VMEM_LIMIT),
        name="reorder_in_proj",
    )(w_in.T)


def _mixer(x2, nw, w_in, cw, cb, dtb, alog, dskip, snw, w_ssd, scw, w_sc, w_o, batch, seq):
    ts = MIXER_TS
    nt = seq // ts
    w_in_r = _reorder_in_proj(w_in)
    head_of_lane = jnp.arange(SSD_INNER) // SSD_HEAD_DIM
    e1 = (jnp.arange(LANES)[:, None] == head_of_lane[None, :]).astype(BF16)
    e2 = jnp.concatenate([e1, e1], axis=0)
    r = jnp.arange(ts)
    t = _time_of_row(r)
    tril = ((t[:, None] >= t[None, :]) & (r[:, None] // CHUNK == r[None, :] // CHUNK)).astype(BF16)
    dskip_f = jnp.repeat(dskip.astype(F32), SSD_HEAD_DIM).reshape(1, SSD_INNER)

    args = (x2, nw.reshape(1, D_MODEL), w_in_r, cw, cb.reshape(1, SSD_XBC), _pad_lanes(dtb),
            _pad_lanes(alog), dskip_f, snw.reshape(1, SSD_INNER), w_ssd.astype(BF16), scw,
            w_sc.astype(BF16), w_o.astype(BF16), e2, tril)
    in_specs = [pl.BlockSpec((ts, D_MODEL), lambda b, j: (b * nt + j, 0))]
    in_specs += [_const_spec(a.shape) for a in args[1:]]
    scratch = [
        pltpu.VMEM((D_MODEL // LANES, ts, LANES), F32),
        pltpu.VMEM((ts, SSD_XBC), F32),
        pltpu.VMEM((SUBLANES, SSD_XBC), F32),
        pltpu.VMEM((SUBLANES, D_MODEL), F32),
        pltpu.VMEM((ts, SSD_INNER), F32),
        pltpu.VMEM((ts, SSD_INNER), F32),
        pltpu.VMEM((ts, SSD_GROUPS * SSD_STATE), BF16),
        pltpu.VMEM((ts, SSD_GROUPS * SSD_STATE), BF16),
        pltpu.VMEM((ts, SSD_INNER), F32),
        pltpu.VMEM((ts, SSD_INNER), F32),
        pltpu.VMEM((ts, SSD_INNER), F32),
        pltpu.VMEM((ts, SSD_INNER), BF16),
        pltpu.VMEM((ts, D_MODEL), BF16),
        pltpu.VMEM((ts, D_MODEL), F32),
        pltpu.VMEM((ts, D_MODEL), F32),
        pltpu.VMEM((SSD_STATE, SSD_INNER), F32),
    ]
    return pl.pallas_call(
        functools.partial(_mixer_kernel, ts=ts),
        grid=(batch, nt),
        in_specs=in_specs,
        out_specs=pl.BlockSpec((ts, D_MODEL), lambda b, j: (b * nt + j, 0)),
        out_shape=jax.ShapeDtypeStruct(x2.shape, F32),
        scratch_shapes=scratch,
        compiler_params=pltpu.CompilerParams(
            dimension_semantics=("arbitrary", "arbitrary"), vmem_limit_bytes=VMEM_LIMIT),
        name="mixer",
    )(*args)


def _ffn(h, nw, w_gate, w_up, w_down, fw):
    tm = FFN_TM
    t = h.shape[0]
    args = (h, nw.reshape(1, D_MODEL), w_gate.astype(BF16), w_up.astype(BF16), w_down.astype(BF16),
            fw.reshape(1, D_MODEL))
    in_specs = [pl.BlockSpec((tm, D_MODEL), lambda i: (i, 0))]
    in_specs += [_const_spec(a.shape) for a in args[1:]]
    return pl.pallas_call(
        functools.partial(_ffn_kernel, piece=FFN_PIECE, parts=FFN_PARTS),
        grid=(t // tm,),
        in_specs=in_specs,
        out_specs=pl.BlockSpec((tm, D_MODEL), lambda i: (i, 0)),
        out_shape=jax.ShapeDtypeStruct(h.shape, F32),
        compiler_params=pltpu.CompilerParams(
            dimension_semantics=("arbitrary",), vmem_limit_bytes=VMEM_LIMIT),
        name="ffn",
    )(*args)


def kernel(x, norm_mix_w, w_in, ssd_conv_w, ssd_conv_b, dt_bias, a_log, d_skip, ssd_norm_w, w_ssd_proj,
           sconv_w, w_sconv_proj, w_o, norm_ffn_w, w_gate, w_up, w_down, final_norm_w):
    batch, seq, d = x.shape
    assert w_in.shape[0] == 1
    h = x.reshape(batch * seq, d)
    h = _mixer(h, norm_mix_w[0], w_in[0], ssd_conv_w[0], ssd_conv_b[0], dt_bias[0], a_log[0],
               d_skip[0], ssd_norm_w[0], w_ssd_proj[0], sconv_w[0], w_sconv_proj[0], w_o[0],
               batch, seq)
    h = _ffn(h, norm_ffn_w[0], w_gate[0], w_up[0], w_down[0], final_norm_w)
    return h.reshape(batch, seq, d)
```

```python
import functools

import jax
import jax.numpy as jnp
from jax import lax
from jax.experimental import pallas as pl
from jax.experimental.pallas import tpu as pltpu

F32 = jnp.float32
BF16 = jnp.bfloat16

EPS = 1e-6
LOG2E = 1.4426950408889634
D_MODEL = 1024
SSD_INNER = 2048
SSD_HEAD_DIM = 64
SSD_HEADS = 32
SSD_GROUPS = 4
SSD_STATE = 128
SSD_CONV = 4
SSD_XBC = SSD_INNER + 2 * SSD_GROUPS * SSD_STATE
SCONV_K = 3
CHUNK = 64
LANES = 128
SUBLANES = 8
HEADS_PER_BLOCK = 4
BLOCK = HEADS_PER_BLOCK * SSD_HEAD_DIM
N_BLOCKS = SSD_INNER // BLOCK

O_Z = 0
O_XBC = O_Z + SSD_INNER
O_SB = O_XBC + SSD_XBC
O_SC = O_SB + D_MODEL
O_SX = O_SC + D_MODEL
O_GA = O_SX + D_MODEL
O_GB = O_GA + D_MODEL
O_DT = O_GB + D_MODEL

MIXER_TS = 256
SCORES_AHEAD = 1
FFN_TM = 1024
MXU_TILE = 256
FFN_PIECE = 4 * MXU_TILE
FFN_PARTS = 4
VMEM_LIMIT = 58 * 1024 * 1024


def _softplus(v):
    return jnp.maximum(v, 0.0) + jnp.log(1.0 + jnp.exp(-jnp.abs(v)))


def _split2(v):
    hi = v.astype(BF16)
    lo = (v - hi.astype(F32)).astype(BF16)
    return jnp.concatenate([hi, lo], axis=1)


SLAB = 32
PHASES = SLAB // SUBLANES


def _time_of_row(r):
    return (r // SLAB) * SLAB + PHASES * (r % SUBLANES) + (r % SLAB) // SUBLANES


def _to_interleaved(ref, stage_ref):
    n, rows, _ = stage_ref.shape
    for j in range(n):
        stage_ref[j] = ref[:, j * LANES:(j + 1) * LANES]
    return jnp.concatenate(
        [jnp.concatenate([stage_ref[j, pl.ds(s0 + k, SUBLANES, stride=PHASES), :]
                          for s0 in range(0, rows, SLAB) for k in range(PHASES)], axis=0)
         for j in range(n)], axis=1)


def _from_interleaved(val, stage_ref):
    n, rows, _ = stage_ref.shape
    for j in range(n):
        for s0 in range(0, rows, SLAB):
            for k in range(PHASES):
                stage_ref[j, pl.ds(s0 + k, SUBLANES, stride=PHASES), :] = (
                    val[s0 + SUBLANES * k:s0 + SUBLANES * (k + 1), j * LANES:(j + 1) * LANES])


def _shift_rows(s, carry_row):
    ts = s.shape[0]
    last = SLAB - SUBLANES
    tails = jnp.concatenate([s[s0 + last:s0 + SLAB] for s0 in range(0, ts, SLAB)], axis=0)
    rolled = pltpu.roll(tails, 1, axis=0)
    head = rolled[0:SUBLANES]
    row = lax.broadcasted_iota(jnp.int32, head.shape, 0)
    rolled = jnp.concatenate([jnp.where(row == 0, carry_row, head), rolled[SUBLANES:]], axis=0)
    out = []
    for i, s0 in enumerate(range(0, ts, SLAB)):
        out += [rolled[i * SUBLANES:(i + 1) * SUBLANES], s[s0:s0 + last]]
    return jnp.concatenate(out, axis=0)


def _causal_conv(x, w_ref, cols, carry_ref, taps):
    ts = x.shape[0]
    s = w_ref[0:1, cols] * x
    for i in range(1, taps):
        carry = carry_ref[i - 1:i, cols]
        carry_ref[i - 1:i, cols] = s[ts - 1:ts]
        s = w_ref[i:i + 1, cols] * x + _shift_rows(s, carry)
    return s


def _mixer_kernel(x_ref, nw_ref, win_ref, cw_ref, cb_ref, dtb_ref, alog_ref, dskip_ref, snw_ref,
                  wssd_ref, scw_ref, wsc_ref, wo_ref, e2_ref, tril_ref, out_ref,
                  stage_ref, xbc_ref, cc_ref, pc_ref, zs_ref, xs_ref, b_ref, c_ref, af_ref, xdt_ref, y_ref,
                  yn_ref, sbv_ref, ga_ref, gb_ref, state_ref, *, ts):
    j = pl.program_id(1)

    @pl.when(j == 0)
    def _():
        cc_ref[...] = jnp.zeros_like(cc_ref)
        pc_ref[...] = jnp.zeros_like(pc_ref)
        state_ref[...] = jnp.zeros_like(state_ref)

    x = _to_interleaved(x_ref, stage_ref)
    xn = (x * lax.rsqrt(jnp.mean(x * x, axis=-1, keepdims=True) + EPS) * nw_ref[...]).astype(BF16)

    def proj(lo, hi):
        return jnp.dot(xn, win_ref[:, lo:hi], preferred_element_type=F32)

    cw = 256

    def xbc_proj(k):
        xbc_ref[:, k * cw:(k + 1) * cw] = proj(O_XBC + k * cw, O_XBC + (k + 1) * cw)

    xbc_proj(0)
    for k in range(SSD_XBC // cw):
        if k + 1 < SSD_XBC // cw:
            xbc_proj(k + 1)
        cols = slice(k * cw, (k + 1) * cw)
        act = jax.nn.silu(_causal_conv(xbc_ref[:, cols], cw_ref, cols, cc_ref, SSD_CONV) + cb_ref[:, cols])
        if k < SSD_INNER // cw:
            xs_ref[:, cols] = act
        elif k < (SSD_INNER + SSD_GROUPS * SSD_STATE) // cw:
            o = k * cw - SSD_INNER
            b_ref[:, o:o + cw] = act.astype(BF16)
        else:
            o = k * cw - SSD_INNER - SSD_GROUPS * SSD_STATE
            c_ref[:, o:o + cw] = act.astype(BF16)

    for k in range(D_MODEL // cw):
        cols = slice(k * cw, (k + 1) * cw)
        p = proj(O_SC + k * cw, O_SC + (k + 1) * cw) * proj(O_SX + k * cw, O_SX + (k + 1) * cw)
        v = _causal_conv(p, scw_ref, cols, pc_ref, SCONV_K)
        sbv_ref[:, cols] = (proj(O_SB + k * cw, O_SB + (k + 1) * cw) * v).astype(BF16)

    dt = _softplus(proj(O_DT, O_DT + LANES) + dtb_ref[...])
    da = dt * (-jnp.exp(alog_ref[...]))
    hi = da.astype(BF16)
    r1 = da - hi.astype(F32)
    mid = r1.astype(BF16)
    lo = (r1 - mid.astype(F32)).astype(BF16)
    cs = jnp.dot(tril_ref[...], jnp.concatenate([hi, mid, lo], axis=1), preferred_element_type=F32)
    acum = (cs[:, 0:LANES] + cs[:, LANES:2 * LANES] + cs[:, 2 * LANES:3 * LANES]) * LOG2E
    lhs_a = _split2(acum)
    lhs_d = _split2(dt)
    for blk in range(N_BLOCKS):
        cols = slice(blk * BLOCK, (blk + 1) * BLOCK)
        af_ref[:, cols] = jnp.dot(lhs_a, e2_ref[:, cols], preferred_element_type=F32)
        xdt_ref[:, cols] = xs_ref[:, cols] * jnp.dot(lhs_d, e2_ref[:, cols], preferred_element_type=F32)

    rowb = lax.broadcasted_iota(jnp.int32, (CHUNK, BLOCK), 0)
    laneb = lax.broadcasted_iota(jnp.int32, (CHUNK, BLOCK), 1)
    pos = laneb % SSD_HEAD_DIM
    causal = _time_of_row(rowb) >= _time_of_row(pos)
    ident = rowb == pos
    lane1 = lax.broadcasted_iota(jnp.int32, (CHUNK, LANES), 1)
    half_masks = (lane1 < SSD_HEAD_DIM, lane1 >= SSD_HEAD_DIM)
    zero_half = jnp.zeros((CHUNK, LANES), BF16)

    n_chunks = ts // CHUNK
    pw = 256
    pieces = ([(zs_ref, O_Z, k, jax.nn.silu) for k in range(SSD_INNER // pw)]
              + [(ga_ref, O_GA, k, jax.nn.sigmoid) for k in range(D_MODEL // pw)]
              + [(gb_ref, O_GB, k, jax.nn.sigmoid) for k in range(D_MODEL // pw)])
    assert len(pieces) == n_chunks * SSD_GROUPS
    def scores(c, blk, cb4):
        rows = slice(c * CHUNK, (c + 1) * CHUNK)
        cols = slice(blk * BLOCK, (blk + 1) * BLOCK)
        a_f = af_ref[rows, cols]
        a_s = jnp.sum(jnp.where(ident, a_f, 0.0), axis=0, keepdims=True)
        dec = jnp.exp2(jnp.where(causal, a_f - a_s, -jnp.inf))
        return (cb4 * dec).astype(BF16)

    def matmuls(c, blk, sc, c_g, b_g):
        rows = slice(c * CHUNK, (c + 1) * CHUNK)
        cols = slice(blk * BLOCK, (blk + 1) * BLOCK)
        a_f = af_ref[rows, cols]
        xdt = xdt_ref[rows, cols]
        xdt_b = xdt.astype(BF16)
        blocks = []
        for i in range(HEADS_PER_BLOCK):
            live = jnp.where(half_masks[i % 2], xdt_b[:, (i // 2) * LANES:(i // 2 + 1) * LANES], zero_half)
            blocks.append(jnp.concatenate([live, zero_half] if i < 2 else [zero_half, live], axis=1))
        bd = jnp.concatenate(blocks, axis=0)
        y_diag = jnp.dot(sc, bd, preferred_element_type=F32)
        a_last = a_f[CHUNK - 1:CHUNK, :]
        st = state_ref[:, cols]
        y_off = jnp.dot(c_g, st.astype(BF16), preferred_element_type=F32) * jnp.exp2(a_f)
        y_ref[rows, cols] = y_diag + y_off
        wx = (xdt * jnp.exp2(a_last - a_f)).astype(BF16)
        upd = lax.dot_general(b_g, wx, (((0,), (0,)), ((), ())), preferred_element_type=F32)
        state_ref[:, cols] = st * jnp.exp2(a_last) + upd

    def group_inputs(c, g):
        rows = slice(c * CHUNK, (c + 1) * CHUNK)
        gcols = slice(g * SSD_STATE, (g + 1) * SSD_STATE)
        c_g = c_ref[rows, gcols]
        b_g = b_ref[rows, gcols]
        cb2 = lax.dot_general(c_g, jnp.concatenate([b_g, b_g], axis=0),
                              (((1,), (1,)), ((), ())), preferred_element_type=F32)
        return c_g, b_g, jnp.concatenate([cb2, cb2], axis=1)

    order = [(c, g, half) for c in range(n_chunks) for g in range(SSD_GROUPS) for half in range(2)]
    pending = []
    for c, g, half in order:
        if half == 0:
            ref, base, k, fn = pieces[c * SSD_GROUPS + g]
            ref[:, k * pw:(k + 1) * pw] = fn(proj(base + k * pw, base + (k + 1) * pw))
            c_g, b_g, cb4 = group_inputs(c, g)
        blk = g * 2 + half
        pending.append((c, blk, scores(c, blk, cb4), c_g, b_g))
        if len(pending) > SCORES_AHEAD:
            matmuls(*pending.pop(0))
    for p in pending:
        matmuls(*p)

    gw = SSD_INNER // SSD_GROUPS
    for g in range(SSD_GROUPS):
        cols = slice(g * gw, (g + 1) * gw)
        y = (y_ref[:, cols] + dskip_ref[:, cols] * xs_ref[:, cols]) * zs_ref[:, cols]
        r = lax.rsqrt(jnp.mean(y * y, axis=-1, keepdims=True) + EPS)
        yn_ref[:, cols] = (y * r * snw_ref[:, cols]).astype(BF16)
    br_a = jnp.dot(yn_ref[...], wssd_ref[...], preferred_element_type=F32)
    br_b = jnp.dot(sbv_ref[...], wsc_ref[...], preferred_element_type=F32)
    merged = (ga_ref[...] * br_a + gb_ref[...] * br_b).astype(BF16)
    _from_interleaved(jnp.dot(merged, wo_ref[...], preferred_element_type=F32), stage_ref)
    for j in range(D_MODEL // LANES):
        cols = slice(j * LANES, (j + 1) * LANES)
        out_ref[:, cols] = x_ref[:, cols] + stage_ref[j]


def _ffn_kernel(h_ref, nw_ref, wg_ref, wu_ref, wd_ref, fw_ref, out_ref, *, piece, parts):
    hidden = wg_ref.shape[1]
    rows = h_ref.shape[0]
    groups = [slice(i * rows // parts, (i + 1) * rows // parts) for i in range(parts)]
    hs = [h_ref[r, :] for r in groups]
    hns = [(h * lax.rsqrt(jnp.mean(h * h, axis=-1, keepdims=True) + EPS) * nw_ref[...]).astype(BF16) for h in hs]
    accs = list(hs)
    for lo in range(0, hidden, piece):
        cols = slice(lo, min(lo + piece, hidden))
        for i, hn in enumerate(hns):
            g = jnp.dot(hn, wg_ref[:, cols], preferred_element_type=F32)
            u = jnp.dot(hn, wu_ref[:, cols], preferred_element_type=F32)
            act = (jax.nn.silu(g) * u).astype(BF16)
            accs[i] = accs[i] + jnp.dot(act, wd_ref[cols, :], preferred_element_type=F32)
    for r, acc in zip(groups, accs):
        out_ref[r, :] = acc * lax.rsqrt(jnp.mean(acc * acc, axis=-1, keepdims=True) + EPS) * fw_ref[...]


def _const_spec(shape):
    return pl.BlockSpec(shape, lambda *_: (0,) * len(shape), pipeline_mode=pl.Buffered(1))


def _pad_lanes(v):
    return jnp.pad(v.astype(F32), (0, LANES - v.shape[0])).reshape(1, LANES)


def _transpose_kernel(wt_ref, out_ref):
    out_ref[...] = wt_ref[...].T.astype(BF16)


def _reorder_in_proj(w_in):
    d, n = w_in.shape
    cb = 1024
    o_dt = SSD_INNER + SSD_XBC
    n_main = O_DT // cb
    n_head = o_dt // cb

    def src_row(i):
        return jnp.where(i < n_head, i * cb, jnp.where(i < n_main, i * cb + SSD_HEADS, o_dt))

    return pl.pallas_call(
        _transpose_kernel,
        grid=(n_main + 1,),
        in_specs=[pl.BlockSpec((pl.Element(cb), pl.Element(d)), lambda i: (pl.multiple_of(src_row(i), SSD_HEADS), 0))],
        out_specs=pl.BlockSpec((d, cb), lambda i: (0, i)),
        out_shape=jax.ShapeDtypeStruct((d, O_DT + LANES), BF16),
        compiler_params=pltpu.CompilerParams(dimension_semantics=("arbitrary",), vmem_limit_bytes=VMEM_LIMIT),
        name="reorder_in_proj",
    )(w_in.T)


def _mixer(x2, nw, w_in, cw, cb, dtb, alog, dskip, snw, w_ssd, scw, w_sc, w_o, batch, seq):
    ts = MIXER_TS
    nt = seq // ts
    w_in_r = _reorder_in_proj(w_in)
    head_of_lane = jnp.arange(SSD_INNER) // SSD_HEAD_DIM
    e1 = (jnp.arange(LANES)[:, None] == head_of_lane[None, :]).astype(BF16)
    e2 = jnp.concatenate([e1, e1], axis=0)
    r = jnp.arange(ts)
    t = _time_of_row(r)
    tril = ((t[:, None] >= t[None, :]) & (r[:, None] // CHUNK == r[None, :] // CHUNK)).astype(BF16)
    dskip_f = jnp.repeat(dskip.astype(F32), SSD_HEAD_DIM).reshape(1, SSD_INNER)

    args = (x2, nw.reshape(1, D_MODEL), w_in_r, cw, cb.reshape(1, SSD_XBC), _pad_lanes(dtb),
            _pad_lanes(alog), dskip_f, snw.reshape(1, SSD_INNER), w_ssd.astype(BF16), scw,
            w_sc.astype(BF16), w_o.astype(BF16), e2, tril)
    in_specs = [pl.BlockSpec((ts, D_MODEL), lambda b, j: (b * nt + j, 0))]
    in_specs += [_const_spec(a.shape) for a in args[1:]]
    scratch = [
        pltpu.VMEM((D_MODEL // LANES, ts, LANES), F32),
        pltpu.VMEM((ts, SSD_XBC), F32),
        pltpu.VMEM((SUBLANES, SSD_XBC), F32),
        pltpu.VMEM((SUBLANES, D_MODEL), F32),
        pltpu.VMEM((ts, SSD_INNER), F32),
        pltpu.VMEM((ts, SSD_INNER), F32),
        pltpu.VMEM((ts, SSD_GROUPS * SSD_STATE), BF16),
        pltpu.VMEM((ts, SSD_GROUPS * SSD_STATE), BF16),
        pltpu.VMEM((ts, SSD_INNER), F32),
        pltpu.VMEM((ts, SSD_INNER), F32),
        pltpu.VMEM((ts, SSD_INNER), F32),
        pltpu.VMEM((ts, SSD_INNER), BF16),
        pltpu.VMEM((ts, D_MODEL), BF16),
        pltpu.VMEM((ts, D_MODEL), F32),
        pltpu.VMEM((ts, D_MODEL), F32),
        pltpu.VMEM((SSD_STATE, SSD_INNER), F32),
    ]
    return pl.pallas_call(
        functools.partial(_mixer_kernel, ts=ts),
        grid=(batch, nt),
        in_specs=in_specs,
        out_specs=pl.BlockSpec((ts, D_MODEL), lambda b, j: (b * nt + j, 0)),
        out_shape=jax.ShapeDtypeStruct(x2.shape, F32),
        scratch_shapes=scratch,
        compiler_params=pltpu.CompilerParams(
            dimension_semantics=("arbitrary", "arbitrary"), vmem_limit_bytes=VMEM_LIMIT),
        name="mixer",
    )(*args)


def _ffn(h, nw, w_gate, w_up, w_down, fw):
    tm = FFN_TM
    t = h.shape[0]
    args = (h, nw.reshape(1, D_MODEL), w_gate.astype(BF16), w_up.astype(BF16), w_down.astype(BF16),
            fw.reshape(1, D_MODEL))
    in_specs = [pl.BlockSpec((tm, D_MODEL), lambda i: (i, 0))]
    in_specs += [_const_spec(a.shape) for a in args[1:]]
    return pl.pallas_call(
        functools.partial(_ffn_kernel, piece=FFN_PIECE, parts=FFN_PARTS),
        grid=(t // tm,),
        in_specs=in_specs,
        out_specs=pl.BlockSpec((tm, D_MODEL), lambda i: (i, 0)),
        out_shape=jax.ShapeDtypeStruct(h.shape, F32),
        compiler_params=pltpu.CompilerParams(
            dimension_semantics=("arbitrary",), vmem_limit_bytes=VMEM_LIMIT),
        name="ffn",
    )(*args)


def kernel(x, norm_mix_w, w_in, ssd_conv_w, ssd_conv_b, dt_bias, a_log, d_skip, ssd_norm_w, w_ssd_proj,
           sconv_w, w_sconv_proj, w_o, norm_ffn_w, w_gate, w_up, w_down, final_norm_w):
    batch, seq, d = x.shape
    assert w_in.shape[0] == 1
    h = x.reshape(batch * seq, d)
    h = _mixer(h, norm_mix_w[0], w_in[0], ssd_conv_w[0], ssd_conv_b[0], dt_bias[0], a_log[0],
               d_skip[0], ssd_norm_w[0], w_ssd_proj[0], sconv_w[0], w_sconv_proj[0], w_o[0],
               batch, seq)
    h = _ffn(h, norm_ffn_w[0], w_gate[0], w_up[0], w_down[0], final_norm_w)
    return h.reshape(batch, seq, d)
```

```python
import functools

import jax
import jax.numpy as jnp
from jax import lax
from jax.experimental import pallas as pl
from jax.experimental.pallas import tpu as pltpu

F32 = jnp.float32
BF16 = jnp.bfloat16

EPS = 1e-6
LOG2E = 1.4426950408889634
D_MODEL = 1024
SSD_INNER = 2048
SSD_HEAD_DIM = 64
SSD_HEADS = 32
SSD_GROUPS = 4
SSD_STATE = 128
SSD_CONV = 4
SSD_XBC = SSD_INNER + 2 * SSD_GROUPS * SSD_STATE
SCONV_K = 3
CHUNK = 64
LANES = 128
SUBLANES = 8
HEADS_PER_BLOCK = 4
BLOCK = HEADS_PER_BLOCK * SSD_HEAD_DIM
N_BLOCKS = SSD_INNER // BLOCK

O_Z = 0
O_XBC = O_Z + SSD_INNER
O_SB = O_XBC + SSD_XBC
O_SC = O_SB + D_MODEL
O_SX = O_SC + D_MODEL
O_GA = O_SX + D_MODEL
O_GB = O_GA + D_MODEL
O_DT = O_GB + D_MODEL

MIXER_TS = 256
SCORES_AHEAD = 1
FFN_TM = 1024
MXU_TILE = 256
FFN_PIECE = 4 * MXU_TILE
FFN_PARTS = 4
VMEM_LIMIT = 58 * 1024 * 1024


def _softplus(v):
    return jnp.maximum(v, 0.0) + jnp.log(1.0 + jnp.exp(-jnp.abs(v)))


def _split2(v):
    hi = v.astype(BF16)
    lo = (v - hi.astype(F32)).astype(BF16)
    return jnp.concatenate([hi, lo], axis=1)


SLAB = 32
PHASES = SLAB // SUBLANES


def _time_of_row(r):
    return (r // SLAB) * SLAB + PHASES * (r % SUBLANES) + (r % SLAB) // SUBLANES


def _to_interleaved(ref, stage_ref):
    n, rows, _ = stage_ref.shape
    for j in range(n):
        stage_ref[j] = ref[:, j * LANES:(j + 1) * LANES]
    return jnp.concatenate(
        [jnp.concatenate([stage_ref[j, pl.ds(s0 + k, SUBLANES, stride=PHASES), :]
                          for s0 in range(0, rows, SLAB) for k in range(PHASES)], axis=0)
         for j in range(n)], axis=1)


def _from_interleaved(val, stage_ref):
    n, rows, _ = stage_ref.shape
    for j in range(n):
        for s0 in range(0, rows, SLAB):
            for k in range(PHASES):
                stage_ref[j, pl.ds(s0 + k, SUBLANES, stride=PHASES), :] = (
                    val[s0 + SUBLANES * k:s0 + SUBLANES * (k + 1), j * LANES:(j + 1) * LANES])


def _shift_rows(s, carry_row):
    ts = s.shape[0]
    last = SLAB - SUBLANES
    tails = jnp.concatenate([s[s0 + last:s0 + SLAB] for s0 in range(0, ts, SLAB)], axis=0)
    rolled = pltpu.roll(tails, 1, axis=0)
    head = rolled[0:SUBLANES]
    row = lax.broadcasted_iota(jnp.int32, head.shape, 0)
    rolled = jnp.concatenate([jnp.where(row == 0, carry_row, head), rolled[SUBLANES:]], axis=0)
    out = []
    for i, s0 in enumerate(range(0, ts, SLAB)):
        out += [rolled[i * SUBLANES:(i + 1) * SUBLANES], s[s0:s0 + last]]
    return jnp.concatenate(out, axis=0)


def _causal_conv(x, w_ref, cols, carry_ref, taps):
    ts = x.shape[0]
    s = w_ref[0:1, cols] * x
    for i in range(1, taps):
        carry = carry_ref[i - 1:i, cols]
        carry_ref[i - 1:i, cols] = s[ts - 1:ts]
        s = w_ref[i:i + 1, cols] * x + _shift_rows(s, carry)
    return s


def _mixer_kernel(x_ref, nw_ref, win_ref, cw_ref, cb_ref, dtb_ref, alog_ref, dskip_ref, snw_ref,
                  wssd_ref, scw_ref, wsc_ref, wo_ref, e2_ref, tril_ref, out_ref,
                  stage_ref, xbc_ref, cc_ref, pc_ref, zs_ref, xs_ref, b_ref, c_ref, af_ref, xdt_ref, y_ref,
                  yn_ref, sbv_ref, ga_ref, gb_ref, state_ref, *, ts):
    j = pl.program_id(1)

    @pl.when(j == 0)
    def _():
        cc_ref[...] = jnp.zeros_like(cc_ref)
        pc_ref[...] = jnp.zeros_like(pc_ref)
        state_ref[...] = jnp.zeros_like(state_ref)

    x = _to_interleaved(x_ref, stage_ref)
    xn = (x * lax.rsqrt(jnp.mean(x * x, axis=-1, keepdims=True) + EPS) * nw_ref[...]).astype(BF16)

    def proj(lo, hi):
        return jnp.dot(xn, win_ref[:, lo:hi], preferred_element_type=F32)

    dt = _softplus(proj(O_DT, O_DT + LANES) + dtb_ref[...])
    da = dt * (-jnp.exp(alog_ref[...]))
    hi = da.astype(BF16)
    r1 = da - hi.astype(F32)
    mid = r1.astype(BF16)
    lo = (r1 - mid.astype(F32)).astype(BF16)
    cs = jnp.dot(tril_ref[...], jnp.concatenate([hi, mid, lo], axis=1), preferred_element_type=F32)
    acum = (cs[:, 0:LANES] + cs[:, LANES:2 * LANES] + cs[:, 2 * LANES:3 * LANES]) * LOG2E
    lhs_a = _split2(acum)
    lhs_d = _split2(dt)

    cw = 256
    assert cw == BLOCK

    def xbc_proj(k):
        xbc_ref[:, k * cw:(k + 1) * cw] = proj(O_XBC + k * cw, O_XBC + (k + 1) * cw)

    xbc_proj(0)
    for k in range(SSD_XBC // cw):
        if k + 1 < SSD_XBC // cw:
            xbc_proj(k + 1)
        cols = slice(k * cw, (k + 1) * cw)
        act = jax.nn.silu(_causal_conv(xbc_ref[:, cols], cw_ref, cols, cc_ref, SSD_CONV) + cb_ref[:, cols])
        if k < SSD_INNER // cw:
            xs_ref[:, cols] = act
            af_ref[:, cols] = jnp.dot(lhs_a, e2_ref[:, cols], preferred_element_type=F32)
            xdt_ref[:, cols] = act * jnp.dot(lhs_d, e2_ref[:, cols], preferred_element_type=F32)
        elif k < (SSD_INNER + SSD_GROUPS * SSD_STATE) // cw:
            o = k * cw - SSD_INNER
            b_ref[:, o:o + cw] = act.astype(BF16)
        else:
            o = k * cw - SSD_INNER - SSD_GROUPS * SSD_STATE
            c_ref[:, o:o + cw] = act.astype(BF16)

    for k in range(D_MODEL // cw):
        cols = slice(k * cw, (k + 1) * cw)
        p = proj(O_SC + k * cw, O_SC + (k + 1) * cw) * proj(O_SX + k * cw, O_SX + (k + 1) * cw)
        v = _causal_conv(p, scw_ref, cols, pc_ref, SCONV_K)
        sbv_ref[:, cols] = (proj(O_SB + k * cw, O_SB + (k + 1) * cw) * v).astype(BF16)

    rowb = lax.broadcasted_iota(jnp.int32, (CHUNK, BLOCK), 0)
    laneb = lax.broadcasted_iota(jnp.int32, (CHUNK, BLOCK), 1)
    pos = laneb % SSD_HEAD_DIM
    causal = _time_of_row(rowb) >= _time_of_row(pos)
    ident = rowb == pos
    lane1 = lax.broadcasted_iota(jnp.int32, (CHUNK, LANES), 1)
    half_masks = (lane1 < SSD_HEAD_DIM, lane1 >= SSD_HEAD_DIM)
    zero_half = jnp.zeros((CHUNK, LANES), BF16)

    n_chunks = ts // CHUNK
    pw = 256
    pieces = ([(zs_ref, O_Z, k, jax.nn.silu) for k in range(SSD_INNER // pw)]
              + [(ga_ref, O_GA, k, jax.nn.sigmoid) for k in range(D_MODEL // pw)]
              + [(gb_ref, O_GB, k, jax.nn.sigmoid) for k in range(D_MODEL // pw)])
    assert len(pieces) == n_chunks * SSD_GROUPS
    def scores(c, blk, cb4):
        rows = slice(c * CHUNK, (c + 1) * CHUNK)
        cols = slice(blk * BLOCK, (blk + 1) * BLOCK)
        a_f = af_ref[rows, cols]
        a_s = jnp.sum(jnp.where(ident, a_f, 0.0), axis=0, keepdims=True)
        dec = jnp.exp2(jnp.where(causal, a_f - a_s, -jnp.inf))
        return (cb4 * dec).astype(BF16)

    def matmuls(c, blk, sc, c_g, b_g):
        rows = slice(c * CHUNK, (c + 1) * CHUNK)
        cols = slice(blk * BLOCK, (blk + 1) * BLOCK)
        a_f = af_ref[rows, cols]
        xdt = xdt_ref[rows, cols]
        xdt_b = xdt.astype(BF16)
        blocks = []
        for i in range(HEADS_PER_BLOCK):
            live = jnp.where(half_masks[i % 2], xdt_b[:, (i // 2) * LANES:(i // 2 + 1) * LANES], zero_half)
            blocks.append(jnp.concatenate([live, zero_half] if i < 2 else [zero_half, live], axis=1))
        bd = jnp.concatenate(blocks, axis=0)
        y_diag = jnp.dot(sc, bd, preferred_element_type=F32)
        a_last = a_f[CHUNK - 1:CHUNK, :]
        st = state_ref[:, cols]
        y_off = jnp.dot(c_g, st.astype(BF16), preferred_element_type=F32) * jnp.exp2(a_f)
        y_ref[rows, cols] = y_diag + y_off
        wx = (xdt * jnp.exp2(a_last - a_f)).astype(BF16)
        upd = lax.dot_general(b_g, wx, (((0,), (0,)), ((), ())), preferred_element_type=F32)
        state_ref[:, cols] = st * jnp.exp2(a_last) + upd

    def group_inputs(c, g):
        rows = slice(c * CHUNK, (c + 1) * CHUNK)
        gcols = slice(g * SSD_STATE, (g + 1) * SSD_STATE)
        c_g = c_ref[rows, gcols]
        b_g = b_ref[rows, gcols]
        cb2 = lax.dot_general(c_g, jnp.concatenate([b_g, b_g], axis=0),
                              (((1,), (1,)), ((), ())), preferred_element_type=F32)
        return c_g, b_g, jnp.concatenate([cb2, cb2], axis=1)

    order = [(c, g, half) for c in range(n_chunks) for g in range(SSD_GROUPS) for half in range(2)]
    pending = []
    for c, g, half in order:
        if half == 0:
            ref, base, k, fn = pieces[c * SSD_GROUPS + g]
            ref[:, k * pw:(k + 1) * pw] = fn(proj(base + k * pw, base + (k + 1) * pw))
            c_g, b_g, cb4 = group_inputs(c, g)
        blk = g * 2 + half
        pending.append((c, blk, scores(c, blk, cb4), c_g, b_g))
        if len(pending) > SCORES_AHEAD:
            matmuls(*pending.pop(0))
    for p in pending:
        matmuls(*p)

    gw = SSD_INNER // SSD_GROUPS
    for g in range(SSD_GROUPS):
        cols = slice(g * gw, (g + 1) * gw)
        y = (y_ref[:, cols] + dskip_ref[:, cols] * xs_ref[:, cols]) * zs_ref[:, cols]
        r = lax.rsqrt(jnp.mean(y * y, axis=-1, keepdims=True) + EPS)
        yn_ref[:, cols] = (y * r * snw_ref[:, cols]).astype(BF16)
    br_a = jnp.dot(yn_ref[...], wssd_ref[...], preferred_element_type=F32)
    br_b = jnp.dot(sbv_ref[...], wsc_ref[...], preferred_element_type=F32)
    merged = (ga_ref[...] * br_a + gb_ref[...] * br_b).astype(BF16)
    _from_interleaved(jnp.dot(merged, wo_ref[...], preferred_element_type=F32), stage_ref)
    for j in range(D_MODEL // LANES):
        cols = slice(j * LANES, (j + 1) * LANES)
        out_ref[:, cols] = x_ref[:, cols] + stage_ref[j]


def _ffn_kernel(h_ref, nw_ref, wg_ref, wu_ref, wd_ref, fw_ref, out_ref, *, piece, parts):
    hidden = wg_ref.shape[1]
    rows = h_ref.shape[0]
    groups = [slice(i * rows // parts, (i + 1) * rows // parts) for i in range(parts)]
    hs = [h_ref[r, :] for r in groups]
    hns = [(h * lax.rsqrt(jnp.mean(h * h, axis=-1, keepdims=True) + EPS) * nw_ref[...]).astype(BF16) for h in hs]
    accs = list(hs)
    for lo in range(0, hidden, piece):
        cols = slice(lo, min(lo + piece, hidden))
        for i, hn in enumerate(hns):
            g = jnp.dot(hn, wg_ref[:, cols], preferred_element_type=F32)
            u = jnp.dot(hn, wu_ref[:, cols], preferred_element_type=F32)
            act = (jax.nn.silu(g) * u).astype(BF16)
            accs[i] = accs[i] + jnp.dot(act, wd_ref[cols, :], preferred_element_type=F32)
    for r, acc in zip(groups, accs):
        out_ref[r, :] = acc * lax.rsqrt(jnp.mean(acc * acc, axis=-1, keepdims=True) + EPS) * fw_ref[...]


def _const_spec(shape):
    return pl.BlockSpec(shape, lambda *_: (0,) * len(shape), pipeline_mode=pl.Buffered(1))


def _pad_lanes(v):
    return jnp.pad(v.astype(F32), (0, LANES - v.shape[0])).reshape(1, LANES)


def _transpose_kernel(wt_ref, out_ref):
    out_ref[...] = wt_ref[...].T.astype(BF16)


def _reorder_in_proj(w_in):
    d, n = w_in.shape
    cb = 1024
    o_dt = SSD_INNER + SSD_XBC
    n_main = O_DT // cb
    n_head = o_dt // cb

    def src_row(i):
        return jnp.where(i < n_head, i * cb, jnp.where(i < n_main, i * cb + SSD_HEADS, o_dt))

    return pl.pallas_call(
        _transpose_kernel,
        grid=(n_main + 1,),
        in_specs=[pl.BlockSpec((pl.Element(cb), pl.Element(d)), lambda i: (pl.multiple_of(src_row(i), SSD_HEADS), 0))],
        out_specs=pl.BlockSpec((d, cb), lambda i: (0, i)),
        out_shape=jax.ShapeDtypeStruct((d, O_DT + LANES), BF16),
        compiler_params=pltpu.CompilerParams(dimension_semantics=("arbitrary",), vmem_limit_bytes=VMEM_LIMIT),
        name="reorder_in_proj",
    )(w_in.T)


def _mixer(x2, nw, w_in, cw, cb, dtb, alog, dskip, snw, w_ssd, scw, w_sc, w_o, batch, seq):
    ts = MIXER_TS
    nt = seq // ts
    w_in_r = _reorder_in_proj(w_in)
    head_of_lane = jnp.arange(SSD_INNER) // SSD_HEAD_DIM
    e1 = (jnp.arange(LANES)[:, None] == head_of_lane[None, :]).astype(BF16)
    e2 = jnp.concatenate([e1, e1], axis=0)
    r = jnp.arange(ts)
    t = _time_of_row(r)
    tril = ((t[:, None] >= t[None, :]) & (r[:, None] // CHUNK == r[None, :] // CHUNK)).astype(BF16)
    dskip_f = jnp.repeat(dskip.astype(F32), SSD_HEAD_DIM).reshape(1, SSD_INNER)

    args = (x2, nw.reshape(1, D_MODEL), w_in_r, cw, cb.reshape(1, SSD_XBC), _pad_lanes(dtb),
            _pad_lanes(alog), dskip_f, snw.reshape(1, SSD_INNER), w_ssd.astype(BF16), scw,
            w_sc.astype(BF16), w_o.astype(BF16), e2, tril)
    in_specs = [pl.BlockSpec((ts, D_MODEL), lambda b, j: (b * nt + j, 0))]
    in_specs += [_const_spec(a.shape) for a in args[1:]]
    scratch = [
        pltpu.VMEM((D_MODEL // LANES, ts, LANES), F32),
        pltpu.VMEM((ts, SSD_XBC), F32),
        pltpu.VMEM((SUBLANES, SSD_XBC), F32),
        pltpu.VMEM((SUBLANES, D_MODEL), F32),
        pltpu.VMEM((ts, SSD_INNER), F32),
        pltpu.VMEM((ts, SSD_INNER), F32),
        pltpu.VMEM((ts, SSD_GROUPS * SSD_STATE), BF16),
        pltpu.VMEM((ts, SSD_GROUPS * SSD_STATE), BF16),
        pltpu.VMEM((ts, SSD_INNER), F32),
        pltpu.VMEM((ts, SSD_INNER), F32),
        pltpu.VMEM((ts, SSD_INNER), F32),
        pltpu.VMEM((ts, SSD_INNER), BF16),
        pltpu.VMEM((ts, D_MODEL), BF16),
        pltpu.VMEM((ts, D_MODEL), F32),
        pltpu.VMEM((ts, D_MODEL), F32),
        pltpu.VMEM((SSD_STATE, SSD_INNER), F32),
    ]
    return pl.pallas_call(
        functools.partial(_mixer_kernel, ts=ts),
        grid=(batch, nt),
        in_specs=in_specs,
        out_specs=pl.BlockSpec((ts, D_MODEL), lambda b, j: (b * nt + j, 0)),
        out_shape=jax.ShapeDtypeStruct(x2.shape, F32),
        scratch_shapes=scratch,
        compiler_params=pltpu.CompilerParams(
            dimension_semantics=("arbitrary", "arbitrary"), vmem_limit_bytes=VMEM_LIMIT),
        name="mixer",
    )(*args)


def _ffn(h, nw, w_gate, w_up, w_down, fw):
    tm = FFN_TM
    t = h.shape[0]
    args = (h, nw.reshape(1, D_MODEL), w_gate.astype(BF16), w_up.astype(BF16), w_down.astype(BF16),
            fw.reshape(1, D_MODEL))
    in_specs = [pl.BlockSpec((tm, D_MODEL), lambda i: (i, 0))]
    in_specs += [_const_spec(a.shape) for a in args[1:]]
    return pl.pallas_call(
        functools.partial(_ffn_kernel, piece=FFN_PIECE, parts=FFN_PARTS),
        grid=(t // tm,),
        in_specs=in_specs,
        out_specs=pl.BlockSpec((tm, D_MODEL), lambda i: (i, 0)),
        out_shape=jax.ShapeDtypeStruct(h.shape, F32),
        compiler_params=pltpu.CompilerParams(
            dimension_semantics=("arbitrary",), vmem_limit_bytes=VMEM_LIMIT),
        name="ffn",
    )(*args)


def kernel(x, norm_mix_w, w_in, ssd_conv_w, ssd_conv_b, dt_bias, a_log, d_skip, ssd_norm_w, w_ssd_proj,
           sconv_w, w_sconv_proj, w_o, norm_ffn_w, w_gate, w_up, w_down, final_norm_w):
    batch, seq, d = x.shape
    assert w_in.shape[0] == 1
    h = x.reshape(batch * seq, d)
    h = _mixer(h, norm_mix_w[0], w_in[0], ssd_conv_w[0], ssd_conv_b[0], dt_bias[0], a_log[0],
               d_skip[0], ssd_norm_w[0], w_ssd_proj[0], sconv_w[0], w_sconv_proj[0], w_o[0],
               batch, seq)
    h = _ffn(h, norm_ffn_w[0], w_gate[0], w_up[0], w_down[0], final_norm_w)
    return h.reshape(batch, seq, d)
```

```python
import functools

import jax
import jax.numpy as jnp
from jax import lax
from jax.experimental import pallas as pl
from jax.experimental.pallas import tpu as pltpu

F32 = jnp.float32
BF16 = jnp.bfloat16

EPS = 1e-6
LOG2E = 1.4426950408889634
D_MODEL = 1024
SSD_INNER = 2048
SSD_HEAD_DIM = 64
SSD_HEADS = 32
SSD_GROUPS = 4
SSD_STATE = 128
SSD_CONV = 4
SSD_XBC = SSD_INNER + 2 * SSD_GROUPS * SSD_STATE
SCONV_K = 3
CHUNK = 64
LANES = 128
SUBLANES = 8
HEADS_PER_BLOCK = 4
BLOCK = HEADS_PER_BLOCK * SSD_HEAD_DIM
N_BLOCKS = SSD_INNER // BLOCK

O_Z = 0
O_XBC = O_Z + SSD_INNER
O_SB = O_XBC + SSD_XBC
O_SC = O_SB + D_MODEL
O_SX = O_SC + D_MODEL
O_GA = O_SX + D_MODEL
O_GB = O_GA + D_MODEL
O_DT = O_GB + D_MODEL

MIXER_TS = 256
SCORES_AHEAD = 1
FFN_TM = 1024
MXU_TILE = 256
FFN_PIECE = 4 * MXU_TILE
FFN_PARTS = 4
VMEM_LIMIT = 58 * 1024 * 1024


def _softplus(v):
    return jnp.maximum(v, 0.0) + jnp.log(1.0 + jnp.exp(-jnp.abs(v)))


def _split2(v):
    hi = v.astype(BF16)
    lo = (v - hi.astype(F32)).astype(BF16)
    return jnp.concatenate([hi, lo], axis=1)


SLAB = 32
PHASES = SLAB // SUBLANES


def _time_of_row(r):
    return (r // SLAB) * SLAB + PHASES * (r % SUBLANES) + (r % SLAB) // SUBLANES


def _to_interleaved(ref, stage_ref):
    n, rows, _ = stage_ref.shape
    for j in range(n):
        stage_ref[j] = ref[:, j * LANES:(j + 1) * LANES]
    return jnp.concatenate(
        [jnp.concatenate([stage_ref[j, pl.ds(s0 + k, SUBLANES, stride=PHASES), :]
                          for s0 in range(0, rows, SLAB) for k in range(PHASES)], axis=0)
         for j in range(n)], axis=1)


def _from_interleaved(val, stage_ref):
    n, rows, _ = stage_ref.shape
    for j in range(n):
        for s0 in range(0, rows, SLAB):
            for k in range(PHASES):
                stage_ref[j, pl.ds(s0 + k, SUBLANES, stride=PHASES), :] = (
                    val[s0 + SUBLANES * k:s0 + SUBLANES * (k + 1), j * LANES:(j + 1) * LANES])


def _shift_rows(s, carry_row):
    ts = s.shape[0]
    last = SLAB - SUBLANES
    tails = jnp.concatenate([s[s0 + last:s0 + SLAB] for s0 in range(0, ts, SLAB)], axis=0)
    rolled = pltpu.roll(tails, 1, axis=0)
    head = rolled[0:SUBLANES]
    row = lax.broadcasted_iota(jnp.int32, head.shape, 0)
    rolled = jnp.concatenate([jnp.where(row == 0, carry_row, head), rolled[SUBLANES:]], axis=0)
    out = []
    for i, s0 in enumerate(range(0, ts, SLAB)):
        out += [rolled[i * SUBLANES:(i + 1) * SUBLANES], s[s0:s0 + last]]
    return jnp.concatenate(out, axis=0)


def _causal_conv(x, w_ref, cols, carry_ref, taps):
    ts = x.shape[0]
    s = w_ref[0:1, cols] * x
    for i in range(1, taps):
        carry = carry_ref[i - 1:i, cols]
        carry_ref[i - 1:i, cols] = s[ts - 1:ts]
        s = w_ref[i:i + 1, cols] * x + _shift_rows(s, carry)
    return s


def _mixer_kernel(x_ref, nw_ref, win_ref, cw_ref, cb_ref, dtb_ref, alog_ref, dskip_ref, snw_ref,
                  wssd_ref, scw_ref, wsc_ref, wo_ref, e2_ref, tril_ref, out_ref,
                  stage_ref, cc_ref, pc_ref, zs_ref, xs_ref, b_ref, c_ref, af_ref, xdt_ref, y_ref,
                  yn_ref, sbv_ref, ga_ref, gb_ref, state_ref, *, ts):
    j = pl.program_id(1)

    @pl.when(j == 0)
    def _():
        cc_ref[...] = jnp.zeros_like(cc_ref)
        pc_ref[...] = jnp.zeros_like(pc_ref)
        state_ref[...] = jnp.zeros_like(state_ref)

    x = _to_interleaved(x_ref, stage_ref)
    xn = (x * lax.rsqrt(jnp.mean(x * x, axis=-1, keepdims=True) + EPS) * nw_ref[...]).astype(BF16)

    def proj(lo, hi):
        return jnp.dot(xn, win_ref[:, lo:hi], preferred_element_type=F32)

    dt = _softplus(proj(O_DT, O_DT + LANES) + dtb_ref[...])
    da = dt * (-jnp.exp(alog_ref[...]))
    hi = da.astype(BF16)
    r1 = da - hi.astype(F32)
    mid = r1.astype(BF16)
    lo = (r1 - mid.astype(F32)).astype(BF16)
    cs = jnp.dot(tril_ref[...], jnp.concatenate([hi, mid, lo], axis=1), preferred_element_type=F32)
    acum = (cs[:, 0:LANES] + cs[:, LANES:2 * LANES] + cs[:, 2 * LANES:3 * LANES]) * LOG2E
    lhs_a = _split2(acum)
    lhs_d = _split2(dt)

    cw = 256
    assert cw == BLOCK

    def xbc_proj(k):
        return proj(O_XBC + k * cw, O_XBC + (k + 1) * cw)

    nxt = xbc_proj(0)
    for k in range(SSD_XBC // cw):
        cur = nxt
        if k + 1 < SSD_XBC // cw:
            nxt = xbc_proj(k + 1)
        cols = slice(k * cw, (k + 1) * cw)
        act = jax.nn.silu(_causal_conv(cur, cw_ref, cols, cc_ref, SSD_CONV) + cb_ref[:, cols])
        if k < SSD_INNER // cw:
            xs_ref[:, cols] = act
            af_ref[:, cols] = jnp.dot(lhs_a, e2_ref[:, cols], preferred_element_type=F32)
            xdt_ref[:, cols] = act * jnp.dot(lhs_d, e2_ref[:, cols], preferred_element_type=F32)
        elif k < (SSD_INNER + SSD_GROUPS * SSD_STATE) // cw:
            o = k * cw - SSD_INNER
            b_ref[:, o:o + cw] = act.astype(BF16)
        else:
            o = k * cw - SSD_INNER - SSD_GROUPS * SSD_STATE
            c_ref[:, o:o + cw] = act.astype(BF16)

    for k in range(D_MODEL // cw):
        cols = slice(k * cw, (k + 1) * cw)
        p = proj(O_SC + k * cw, O_SC + (k + 1) * cw) * proj(O_SX + k * cw, O_SX + (k + 1) * cw)
        v = _causal_conv(p, scw_ref, cols, pc_ref, SCONV_K)
        sbv_ref[:, cols] = (proj(O_SB + k * cw, O_SB + (k + 1) * cw) * v).astype(BF16)

    rowb = lax.broadcasted_iota(jnp.int32, (CHUNK, BLOCK), 0)
    laneb = lax.broadcasted_iota(jnp.int32, (CHUNK, BLOCK), 1)
    pos = laneb % SSD_HEAD_DIM
    causal = _time_of_row(rowb) >= _time_of_row(pos)
    ident = rowb == pos
    lane1 = lax.broadcasted_iota(jnp.int32, (CHUNK, LANES), 1)
    half_masks = (lane1 < SSD_HEAD_DIM, lane1 >= SSD_HEAD_DIM)
    zero_half = jnp.zeros((CHUNK, LANES), BF16)

    n_chunks = ts // CHUNK
    pw = 256
    pieces = ([(zs_ref, O_Z, k, jax.nn.silu) for k in range(SSD_INNER // pw)]
              + [(ga_ref, O_GA, k, jax.nn.sigmoid) for k in range(D_MODEL // pw)]
              + [(gb_ref, O_GB, k, jax.nn.sigmoid) for k in range(D_MODEL // pw)])
    assert len(pieces) == n_chunks * SSD_GROUPS
    def scores(c, blk, cb4):
        rows = slice(c * CHUNK, (c + 1) * CHUNK)
        cols = slice(blk * BLOCK, (blk + 1) * BLOCK)
        a_f = af_ref[rows, cols]
        a_s = jnp.sum(jnp.where(ident, a_f, 0.0), axis=0, keepdims=True)
        dec = jnp.exp2(jnp.where(causal, a_f - a_s, -jnp.inf))
        return (cb4 * dec).astype(BF16)

    def matmuls(c, blk, sc, c_g, b_g):
        rows = slice(c * CHUNK, (c + 1) * CHUNK)
        cols = slice(blk * BLOCK, (blk + 1) * BLOCK)
        a_f = af_ref[rows, cols]
        xdt = xdt_ref[rows, cols]
        xdt_b = xdt.astype(BF16)
        blocks = []
        for i in range(HEADS_PER_BLOCK):
            live = jnp.where(half_masks[i % 2], xdt_b[:, (i // 2) * LANES:(i // 2 + 1) * LANES], zero_half)
            blocks.append(jnp.concatenate([live, zero_half] if i < 2 else [zero_half, live], axis=1))
        bd = jnp.concatenate(blocks, axis=0)
        y_diag = jnp.dot(sc, bd, preferred_element_type=F32)
        a_last = a_f[CHUNK - 1:CHUNK, :]
        st = state_ref[:, cols]
        y_off = jnp.dot(c_g, st.astype(BF16), preferred_element_type=F32) * jnp.exp2(a_f)
        y_ref[rows, cols] = y_diag + y_off
        wx = (xdt * jnp.exp2(a_last - a_f)).astype(BF16)
        upd = lax.dot_general(b_g, wx, (((0,), (0,)), ((), ())), preferred_element_type=F32)
        state_ref[:, cols] = st * jnp.exp2(a_last) + upd

    def group_inputs(c, g):
        rows = slice(c * CHUNK, (c + 1) * CHUNK)
        gcols = slice(g * SSD_STATE, (g + 1) * SSD_STATE)
        c_g = c_ref[rows, gcols]
        b_g = b_ref[rows, gcols]
        cb2 = lax.dot_general(c_g, jnp.concatenate([b_g, b_g], axis=0),
                              (((1,), (1,)), ((), ())), preferred_element_type=F32)
        return c_g, b_g, jnp.concatenate([cb2, cb2], axis=1)

    order = [(c, g, half) for c in range(n_chunks) for g in range(SSD_GROUPS) for half in range(2)]
    pending = []
    for c, g, half in order:
        if half == 0:
            ref, base, k, fn = pieces[c * SSD_GROUPS + g]
            ref[:, k * pw:(k + 1) * pw] = fn(proj(base + k * pw, base + (k + 1) * pw))
            c_g, b_g, cb4 = group_inputs(c, g)
        blk = g * 2 + half
        pending.append((c, blk, scores(c, blk, cb4), c_g, b_g))
        if len(pending) > SCORES_AHEAD:
            matmuls(*pending.pop(0))
    for p in pending:
        matmuls(*p)

    gw = SSD_INNER // SSD_GROUPS
    for g in range(SSD_GROUPS):
        cols = slice(g * gw, (g + 1) * gw)
        y = (y_ref[:, cols] + dskip_ref[:, cols] * xs_ref[:, cols]) * zs_ref[:, cols]
        r = lax.rsqrt(jnp.mean(y * y, axis=-1, keepdims=True) + EPS)
        yn_ref[:, cols] = (y * r * snw_ref[:, cols]).astype(BF16)
    br_a = jnp.dot(yn_ref[...], wssd_ref[...], preferred_element_type=F32)
    br_b = jnp.dot(sbv_ref[...], wsc_ref[...], preferred_element_type=F32)
    merged = (ga_ref[...] * br_a + gb_ref[...] * br_b).astype(BF16)
    _from_interleaved(jnp.dot(merged, wo_ref[...], preferred_element_type=F32), stage_ref)
    for j in range(D_MODEL // LANES):
        cols = slice(j * LANES, (j + 1) * LANES)
        out_ref[:, cols] = x_ref[:, cols] + stage_ref[j]


def _ffn_kernel(h_ref, nw_ref, wg_ref, wu_ref, wd_ref, fw_ref, out_ref, *, piece, parts):
    hidden = wg_ref.shape[1]
    rows = h_ref.shape[0]
    groups = [slice(i * rows // parts, (i + 1) * rows // parts) for i in range(parts)]
    hs = [h_ref[r, :] for r in groups]
    hns = [(h * lax.rsqrt(jnp.mean(h * h, axis=-1, keepdims=True) + EPS) * nw_ref[...]).astype(BF16) for h in hs]
    accs = list(hs)
    for lo in range(0, hidden, piece):
        cols = slice(lo, min(lo + piece, hidden))
        for i, hn in enumerate(hns):
            g = jnp.dot(hn, wg_ref[:, cols], preferred_element_type=F32)
            u = jnp.dot(hn, wu_ref[:, cols], preferred_element_type=F32)
            act = (jax.nn.silu(g) * u).astype(BF16)
            accs[i] = accs[i] + jnp.dot(act, wd_ref[cols, :], preferred_element_type=F32)
    for r, acc in zip(groups, accs):
        out_ref[r, :] = acc * lax.rsqrt(jnp.mean(acc * acc, axis=-1, keepdims=True) + EPS) * fw_ref[...]


def _const_spec(shape):
    return pl.BlockSpec(shape, lambda *_: (0,) * len(shape), pipeline_mode=pl.Buffered(1))


def _pad_lanes(v):
    return jnp.pad(v.astype(F32), (0, LANES - v.shape[0])).reshape(1, LANES)


def _transpose_kernel(wt_ref, out_ref):
    out_ref[...] = wt_ref[...].T.astype(BF16)


def _reorder_in_proj(w_in):
    d, n = w_in.shape
    cb = 1024
    o_dt = SSD_INNER + SSD_XBC
    n_main = O_DT // cb
    n_head = o_dt // cb

    def src_row(i):
        return jnp.where(i < n_head, i * cb, jnp.where(i < n_main, i * cb + SSD_HEADS, o_dt))

    return pl.pallas_call(
        _transpose_kernel,
        grid=(n_main + 1,),
        in_specs=[pl.BlockSpec((pl.Element(cb), pl.Element(d)), lambda i: (pl.multiple_of(src_row(i), SSD_HEADS), 0))],
        out_specs=pl.BlockSpec((d, cb), lambda i: (0, i)),
        out_shape=jax.ShapeDtypeStruct((d, O_DT + LANES), BF16),
        compiler_params=pltpu.CompilerParams(dimension_semantics=("arbitrary",), vmem_limit_bytes=VMEM_LIMIT),
        name="reorder_in_proj",
    )(w_in.T)


def _mixer(x2, nw, w_in, cw, cb, dtb, alog, dskip, snw, w_ssd, scw, w_sc, w_o, batch, seq):
    ts = MIXER_TS
    nt = seq // ts
    w_in_r = _reorder_in_proj(w_in)
    head_of_lane = jnp.arange(SSD_INNER) // SSD_HEAD_DIM
    e1 = (jnp.arange(LANES)[:, None] == head_of_lane[None, :]).astype(BF16)
    e2 = jnp.concatenate([e1, e1], axis=0)
    r = jnp.arange(ts)
    t = _time_of_row(r)
    tril = ((t[:, None] >= t[None, :]) & (r[:, None] // CHUNK == r[None, :] // CHUNK)).astype(BF16)
    dskip_f = jnp.repeat(dskip.astype(F32), SSD_HEAD_DIM).reshape(1, SSD_INNER)

    args = (x2, nw.reshape(1, D_MODEL), w_in_r, cw, cb.reshape(1, SSD_XBC), _pad_lanes(dtb),
            _pad_lanes(alog), dskip_f, snw.reshape(1, SSD_INNER), w_ssd.astype(BF16), scw,
            w_sc.astype(BF16), w_o.astype(BF16), e2, tril)
    in_specs = [pl.BlockSpec((ts, D_MODEL), lambda b, j: (b * nt + j, 0))]
    in_specs += [_const_spec(a.shape) for a in args[1:]]
    scratch = [
        pltpu.VMEM((D_MODEL // LANES, ts, LANES), F32),
        pltpu.VMEM((SUBLANES, SSD_XBC), F32),
        pltpu.VMEM((SUBLANES, D_MODEL), F32),
        pltpu.VMEM((ts, SSD_INNER), F32),
        pltpu.VMEM((ts, SSD_INNER), F32),
        pltpu.VMEM((ts, SSD_GROUPS * SSD_STATE), BF16),
        pltpu.VMEM((ts, SSD_GROUPS * SSD_STATE), BF16),
        pltpu.VMEM((ts, SSD_INNER), F32),
        pltpu.VMEM((ts, SSD_INNER), F32),
        pltpu.VMEM((ts, SSD_INNER), F32),
        pltpu.VMEM((ts, SSD_INNER), BF16),
        pltpu.VMEM((ts, D_MODEL), BF16),
        pltpu.VMEM((ts, D_MODEL), F32),
        pltpu.VMEM((ts, D_MODEL), F32),
        pltpu.VMEM((SSD_STATE, SSD_INNER), F32),
    ]
    return pl.pallas_call(
        functools.partial(_mixer_kernel, ts=ts),
        grid=(batch, nt),
        in_specs=in_specs,
        out_specs=pl.BlockSpec((ts, D_MODEL), lambda b, j: (b * nt + j, 0)),
        out_shape=jax.ShapeDtypeStruct(x2.shape, F32),
        scratch_shapes=scratch,
        compiler_params=pltpu.CompilerParams(
            dimension_semantics=("arbitrary", "arbitrary"), vmem_limit_bytes=VMEM_LIMIT),
        name="mixer",
    )(*args)


def _ffn(h, nw, w_gate, w_up, w_down, fw):
    tm = FFN_TM
    t = h.shape[0]
    args = (h, nw.reshape(1, D_MODEL), w_gate.astype(BF16), w_up.astype(BF16), w_down.astype(BF16),
            fw.reshape(1, D_MODEL))
    in_specs = [pl.BlockSpec((tm, D_MODEL), lambda i: (i, 0))]
    in_specs += [_const_spec(a.shape) for a in args[1:]]
    return pl.pallas_call(
        functools.partial(_ffn_kernel, piece=FFN_PIECE, parts=FFN_PARTS),
        grid=(t // tm,),
        in_specs=in_specs,
        out_specs=pl.BlockSpec((tm, D_MODEL), lambda i: (i, 0)),
        out_shape=jax.ShapeDtypeStruct(h.shape, F32),
        compiler_params=pltpu.CompilerParams(
            dimension_semantics=("arbitrary",), vmem_limit_bytes=VMEM_LIMIT),
        name="ffn",
    )(*args)


def kernel(x, norm_mix_w, w_in, ssd_conv_w, ssd_conv_b, dt_bias, a_log, d_skip, ssd_norm_w, w_ssd_proj,
           sconv_w, w_sconv_proj, w_o, norm_ffn_w, w_gate, w_up, w_down, final_norm_w):
    batch, seq, d = x.shape
    assert w_in.shape[0] == 1
    h = x.reshape(batch * seq, d)
    h = _mixer(h, norm_mix_w[0], w_in[0], ssd_conv_w[0], ssd_conv_b[0], dt_bias[0], a_log[0],
               d_skip[0], ssd_norm_w[0], w_ssd_proj[0], sconv_w[0], w_sconv_proj[0], w_o[0],
               batch, seq)
    h = _ffn(h, norm_ffn_w[0], w_gate[0], w_up[0], w_down[0], final_norm_w)
    return h.reshape(batch, seq, d)
```
